```python
import jax, jax.numpy as jnp
from jax import lax
import numpy as np

D_MODEL = 2048
BATCH = 4
SEQ = 2048
DEPTH = 4

GRID_W = 64
N_HEADS = 64
HEAD_DIM = D_MODEL // N_HEADS
WIN_H = 8
WIN_W = 16
POOL_WINDOWS = (2, 4, 8, 16)
N_GROUPS = len(POOL_WINDOWS)
GROUP_CH = D_MODEL // N_GROUPS
D_FF = ((8 * D_MODEL + 3 * 256 - 1) // (3 * 256)) * 256
PLE_DIM = 256
N_MIXERS = 2
N_ATTN = (DEPTH + 1) // 2
N_POOL = DEPTH // 2
RMS_EPS = 1e-6

kernel_name = "hybrid_natten_poolformer_encoder"


def rms_norm(x, g):
    xf = x.astype(jnp.float32)
    y = xf * lax.rsqrt(jnp.mean(xf * xf, axis=-1, keepdims=True) + RMS_EPS)
    return (y * g.astype(jnp.float32)).astype(x.dtype)


def neighbourhood_attention(h, w_qkv, b_qkv, w_o, rpb):
    B, S, _ = h.shape
    rows = S // GRID_W
    kh = min(WIN_H, rows)
    qkv = (h @ w_qkv + b_qkv).reshape(B, S, 3, N_HEADS, HEAD_DIM)
    q = qkv[:, :, 0] * (HEAD_DIM ** -0.5)
    k = qkv[:, :, 1]
    v = qkv[:, :, 2]

    qc = jnp.arange(GRID_W)[:, None]
    kc = jnp.arange(GRID_W)[None, :]
    c_start = jnp.clip(qc - WIN_W // 2, 0, GRID_W - WIN_W)
    col_mask = (kc >= c_start) & (kc < c_start + WIN_W)
    dc_idx = jnp.clip(kc - qc, -(WIN_W - 1), WIN_W - 1) + (WIN_W - 1)
    mask = jnp.broadcast_to(col_mask[:, None, :], (GRID_W, kh, GRID_W)).reshape(GRID_W, kh * GRID_W)
    neg = jnp.finfo(jnp.float32).min

    def row_block(r):
        r_start = jnp.clip(r - kh // 2, 0, rows - kh)
        qb = lax.dynamic_slice_in_dim(q, r * GRID_W, GRID_W, axis=1)
        kb = lax.dynamic_slice_in_dim(k, r_start * GRID_W, kh * GRID_W, axis=1)
        vb = lax.dynamic_slice_in_dim(v, r_start * GRID_W, kh * GRID_W, axis=1)
        dr_idx = r_start + jnp.arange(kh) - r + (WIN_H - 1)
        bias = rpb[:, dr_idx[None, :, None], dc_idx[:, None, :]]
        bias = bias.reshape(N_HEADS, GRID_W, kh * GRID_W).astype(jnp.float32)
        s = jnp.einsum('bqhd,bkhd->bhqk', qb, kb).astype(jnp.float32) + bias[None]
        s = jnp.where(mask[None, None], s, neg)
        pr = jax.nn.softmax(s, axis=-1).astype(vb.dtype)
        return jnp.einsum('bhqk,bkhd->bqhd', pr, vb)

    o = lax.map(row_block, jnp.arange(rows))
    o = jnp.transpose(o, (1, 0, 2, 3, 4)).reshape(B, S, N_HEADS * HEAD_DIM)
    return o @ w_o


def multiscale_pool(h, w_pool, scale):
    B, S, D = h.shape
    hf = h.astype(jnp.float32)
    csum = jnp.concatenate([jnp.zeros((B, 1, D), jnp.float32), lax.cumsum(hf, axis=1)], axis=1)
    csum = csum.reshape(B, S + 1, N_GROUPS, GROUP_CH)
    t = jnp.arange(S)
    means = []
    for g, w in enumerate(POOL_WINDOWS):
        lo = jnp.clip(t - w // 2, 0, S)
        hi = jnp.clip(t + w - w // 2, 0, S)
        cnt = (hi - lo).astype(jnp.float32)
        cg = csum[:, :, g]
        seg = jnp.take(cg, hi, axis=1) - jnp.take(cg, lo, axis=1)
        means.append(seg / cnt[None, :, None])
    pooled = jnp.stack(means, axis=2)
    mixed = (pooled - hf.reshape(B, S, N_GROUPS, GROUP_CH)).astype(h.dtype)
    out = jnp.einsum('bsgc,gce->bsge', mixed, w_pool).reshape(B, S, D)
    return out * scale


def swiglu(h, w_gate, w_up, w_down):
    return (jax.nn.silu(h @ w_gate) * (h @ w_up)) @ w_down


def setup_inputs(seed: int = 0) -> dict:
    key = jax.random.key(seed)
    ks = jax.random.split(key, 24)
    f32 = jnp.float32
    nrm = lambda k, shape, s: jax.random.normal(k, shape, f32) * s
    gain = lambda k, shape: 1.0 + 0.01 * jax.random.normal(k, shape, f32)
    return {
        "x": nrm(ks[0], (BATCH, SEQ, D_MODEL), 1.0),
        "p": nrm(ks[1], (DEPTH, BATCH, SEQ, PLE_DIM), 1.0),
        "attn_norm_g": gain(ks[2], (N_ATTN, D_MODEL)),
        "w_qkv": nrm(ks[3], (N_ATTN, D_MODEL, 3 * D_MODEL), D_MODEL ** -0.5),
        "b_qkv": nrm(ks[4], (N_ATTN, 3 * D_MODEL), 0.01),
        "w_o": nrm(ks[5], (N_ATTN, D_MODEL, D_MODEL), D_MODEL ** -0.5),
        "rpb": nrm(ks[6], (N_ATTN, N_HEADS, 2 * WIN_H - 1, 2 * WIN_W - 1), 0.1),
        "pool_norm_g": gain(ks[7], (N_POOL, D_MODEL)),
        "w_pool": nrm(ks[8], (N_POOL, N_GROUPS, GROUP_CH, GROUP_CH), GROUP_CH ** -0.5),
        "pool_scale": 1.0 + 0.1 * jax.random.normal(ks[9], (N_POOL, D_MODEL), f32),
        "ffn_norm_g": gain(ks[10], (DEPTH, D_MODEL)),
        "w_gate": nrm(ks[11], (DEPTH, D_MODEL, D_FF), D_MODEL ** -0.5),
        "w_up": nrm(ks[12], (DEPTH, D_MODEL, D_FF), D_MODEL ** -0.5),
        "w_down": nrm(ks[13], (DEPTH, D_FF, D_MODEL), D_FF ** -0.5),
        "ple_norm_g": gain(ks[14], (DEPTH, D_MODEL)),
        "w_ple_gate": nrm(ks[15], (DEPTH, D_MODEL, D_MODEL), D_MODEL ** -0.5),
        "b_ple_gate": nrm(ks[16], (DEPTH, D_MODEL), 0.01),
        "w_ple_proj": nrm(ks[17], (DEPTH, PLE_DIM, D_MODEL), PLE_DIM ** -0.5),
        "final_norm_g": gain(ks[18], (D_MODEL,)),
    }


def reference(x, p, attn_norm_g, w_qkv, b_qkv, w_o, rpb, pool_norm_g, w_pool, pool_scale,
              ffn_norm_g, w_gate, w_up, w_down, ple_norm_g, w_ple_gate, b_ple_gate,
              w_ple_proj, final_norm_g):
    for i in range(DEPTH):
        j = i // N_MIXERS
        if i % N_MIXERS == 0:
            x = x + neighbourhood_attention(rms_norm(x, attn_norm_g[j]), w_qkv[j], b_qkv[j], w_o[j], rpb[j])
        else:
            x = x + multiscale_pool(rms_norm(x, pool_norm_g[j]), w_pool[j], pool_scale[j])
        x = x + swiglu(rms_norm(x, ffn_norm_g[i]), w_gate[i], w_up[i], w_down[i])
        gate = jax.nn.sigmoid(rms_norm(x, ple_norm_g[i]) @ w_ple_gate[i] + b_ple_gate[i])
        x = x + (p[i] @ w_ple_proj[i]) * gate
    return rms_norm(x, final_norm_g)
```

```python
import functools

import jax
import jax.numpy as jnp
import numpy as np
from jax import lax
from jax.experimental import pallas as pl
from jax.experimental.pallas import tpu as pltpu

D_MODEL = 2048
GRID_W = 64
N_HEADS = 64
HEAD_DIM = D_MODEL // N_HEADS
WIN_H = 8
WIN_W = 16
POOL_WINDOWS = (2, 4, 8, 16)
GROUP_CH = D_MODEL // len(POOL_WINDOWS)
RMS_EPS = 1e-6

VMEM_LIMIT_BYTES = 56 * 1024 * 1024

HEADS_PER_GROUP = 8
GROUP_LANES = HEADS_PER_GROUP * HEAD_DIM
KEY_COLS = 32
Q_GROUPS = ((0, 24, 0), (24, 16, 16), (40, 24, 32))
for _q0, _nq, _k0 in Q_GROUPS:
    _cs = np.clip(np.arange(_q0, _q0 + _nq) - WIN_W // 2, 0, GRID_W - WIN_W)
    assert _cs.min() >= _k0 and _cs.max() + WIN_W <= _k0 + KEY_COLS
MASK_BIAS = -1e30

_BF16 = jnp.bfloat16
_F32 = jnp.float32


def _params(*semantics):
    return pltpu.CompilerParams(dimension_semantics=semantics,
                                vmem_limit_bytes=VMEM_LIMIT_BYTES)


def _rms(x, g):
    return x * lax.rsqrt(jnp.mean(x * x, axis=-1, keepdims=True) + RMS_EPS) * g


def _rmsnorm_kernel(x_ref, g_ref, o_ref):
    o_ref[...] = _rms(x_ref[...], g_ref[...]).astype(o_ref.dtype)


def rmsnorm(x, g, out_dtype, tm=512):
    m, d = x.shape
    return pl.pallas_call(
        _rmsnorm_kernel,
        grid=(m // tm,),
        in_specs=[pl.BlockSpec((tm, d), lambda i: (i, 0)),
                  pl.BlockSpec((1, d), lambda i: (0, 0))],
        out_specs=pl.BlockSpec((tm, d), lambda i: (i, 0)),
        out_shape=jax.ShapeDtypeStruct((m, d), out_dtype),
        compiler_params=_params("parallel"),
        name="rmsnorm",
    )(x, g.reshape(1, d))


def _dot(a, b):
    return jnp.dot(a, b, preferred_element_type=_F32)


def _mm_bias_scale_kernel(a_ref, w_ref, b_ref, s_ref, o_ref):
    o_ref[...] = ((_dot(a_ref[...], w_ref[...]) + b_ref[...]) * s_ref[...]).astype(o_ref.dtype)


def _mm_residual_kernel(a_ref, w_ref, r_ref, o_ref):
    o_ref[...] = r_ref[...] + _dot(a_ref[...], w_ref[...])


def _mm_swiglu_kernel(a_ref, wg_ref, wu_ref, o_ref):
    a = a_ref[...]
    gate = _dot(a, wg_ref[...])
    up = _dot(a, wu_ref[...])
    o_ref[...] = (gate * jax.nn.sigmoid(gate) * up).astype(o_ref.dtype)


def _mm_ple_kernel(a_ref, wg_ref, b_ref, p_ref, wp_ref, r_ref, o_ref):
    gate = jax.nn.sigmoid(_dot(a_ref[...], wg_ref[...]) + b_ref[...])
    proj = _dot(p_ref[...].astype(_BF16), wp_ref[...])
    o_ref[...] = r_ref[...] + proj * gate


def _row_spec(tm, k):
    return pl.BlockSpec((tm, k), lambda j, i: (i, 0))


def _col_spec(k, tn):
    return pl.BlockSpec((k, tn), lambda j, i: (0, j))


def _vec_spec(tn):
    return pl.BlockSpec((1, tn), lambda j, i: (0, j))


def _tile_spec(tm, tn):
    return pl.BlockSpec((tm, tn), lambda j, i: (i, j))


def _mm_call(kernel, name, m, n, tm, tn, in_specs, out_dtype):
    return pl.pallas_call(
        kernel,
        grid=(n // tn, m // tm),
        in_specs=in_specs,
        out_specs=_tile_spec(tm, tn),
        out_shape=jax.ShapeDtypeStruct((m, n), out_dtype),
        compiler_params=_params("parallel", "parallel"),
        name=name,
    )


def mm_bias_scale(a, w, b, s, tm=512, tn=512):
    m, k = a.shape
    n = w.shape[1]
    call = _mm_call(_mm_bias_scale_kernel, "mm_bias_scale", m, n, tm, tn,
                    [_row_spec(tm, k), _col_spec(k, tn), _vec_spec(tn), _vec_spec(tn)], _BF16)
    return call(a, w, b.reshape(1, n), s.reshape(1, n))


def mm_residual(a, w, res, tm=512, tn=512):
    m, k = a.shape
    n = w.shape[1]
    call = _mm_call(_mm_residual_kernel, "mm_residual", m, n, tm, tn,
                    [_row_spec(tm, k), _col_spec(k, tn), _tile_spec(tm, tn)], _F32)
    return call(a, w, res)


def mm_swiglu(a, wg, wu, tm=512, tn=512):
    m, k = a.shape
    n = wg.shape[1]
    call = _mm_call(_mm_swiglu_kernel, "mm_swiglu", m, n, tm, tn,
                    [_row_spec(tm, k), _col_spec(k, tn), _col_spec(k, tn)], _BF16)
    return call(a, wg, wu)


def mm_ple(a, wg, b, p, wp, res, tm=512, tn=512):
    m, k = a.shape
    n = wg.shape[1]
    kp = p.shape[1]
    call = _mm_call(_mm_ple_kernel, "mm_ple", m, n, tm, tn,
                    [_row_spec(tm, k), _col_spec(k, tn), _vec_spec(tn),
                     _row_spec(tm, kp), _col_spec(kp, tn), _tile_spec(tm, tn)], _F32)
    return call(a, wg, b.reshape(1, n), p, wp, res)


def _attn_kernel(q_ref, k_ref, v_ref, bias_ref, o_ref):
    rows = q_ref.shape[0] // GRID_W
    lane_head = lax.broadcasted_iota(jnp.int32, (1, GROUP_LANES), 1) // HEAD_DIM
    head_masks = [lane_head == h for h in range(HEADS_PER_GROUP)]

    def row_body(r, carry):
        r_start = jnp.clip(r - WIN_H // 2, 0, rows - WIN_H)
        case = jnp.minimum(r, WIN_H // 2) + jnp.maximum(r - (rows - WIN_H // 2), 0)
        row0 = pl.multiple_of(r * GRID_W, GRID_W)
        q_row = q_ref[pl.ds(row0, GRID_W), :].astype(_F32)
        outs = []
        bias_row = 0
        for q0, nq, k0 in Q_GROUPS:
            qg = q_row[q0:q0 + nq]
            qm = jnp.concatenate([jnp.where(mask, qg, 0.0) for mask in head_masks],
                                 axis=0).astype(_BF16)
            starts = [pl.multiple_of((r_start + j) * GRID_W + k0, 16) for j in range(WIN_H)]
            kwin = jnp.concatenate([k_ref[pl.ds(s, KEY_COLS), :] for s in starts], axis=0)
            vwin = jnp.concatenate([v_ref[pl.ds(s, KEY_COLS), :] for s in starts], axis=0)
            s = lax.dot_general(qm, kwin, (((1,), (1,)), ((), ())),
                                preferred_element_type=_F32)
            s = s + bias_ref[0, case, pl.ds(bias_row, HEADS_PER_GROUP * nq), :]
            bias_row += HEADS_PER_GROUP * nq
            e = jnp.exp(s - jnp.max(s, axis=-1, keepdims=True))
            p = (e / jnp.sum(e, axis=-1, keepdims=True)).astype(_BF16)
            o_all = _dot(p, vwin)
            o = jnp.where(head_masks[0], o_all[:nq], 0.0)
            for h in range(1, HEADS_PER_GROUP):
                o = o + jnp.where(head_masks[h], o_all[h * nq:(h + 1) * nq], 0.0)
            outs.append(o)
        o_ref[pl.ds(row0, GRID_W), :] = jnp.concatenate(outs, axis=0).astype(o_ref.dtype)
        return carry

    lax.fori_loop(0, rows, row_body, 0)


def _attention_bias_table(rpb, rows):
    half = WIN_H // 2
    edge_rows = list(range(half + 1)) + list(range(rows - half + 1, rows))
    case_off = np.array([np.clip(r - half, 0, rows - WIN_H) - r for r in edge_rows])
    dr_idx = case_off[:, None] + np.arange(WIN_H)[None, :] + (WIN_H - 1)
    n_hg = N_HEADS // HEADS_PER_GROUP
    parts = []
    for q0, nq, k0 in Q_GROUPS:
        qc = q0 + np.arange(nq)[:, None]
        kc = k0 + np.arange(KEY_COLS)[None, :]
        c_start = np.clip(qc - WIN_W // 2, 0, GRID_W - WIN_W)
        valid = (kc >= c_start) & (kc < c_start + WIN_W)
        dc_idx = np.clip(kc - qc, -(WIN_W - 1), WIN_W - 1) + (WIN_W - 1)
        part = rpb[:, dr_idx[:, :, None, None], dc_idx[None, None]]
        part = jnp.where(valid[None, None, None], part, MASK_BIAS)
        part = part.reshape(n_hg, HEADS_PER_GROUP, len(edge_rows), WIN_H, nq, KEY_COLS)
        part = jnp.transpose(part, (0, 2, 1, 4, 3, 5))
        parts.append(part.reshape(n_hg, len(edge_rows), HEADS_PER_GROUP * nq, WIN_H * KEY_COLS))
    return jnp.concatenate(parts, axis=2)


def neighbourhood_attention(qkv, bias_tbl, batch, seq):
    n_hg = N_HEADS // HEADS_PER_GROUP
    blk = (seq, GROUP_LANES)
    return pl.pallas_call(
        _attn_kernel,
        grid=(n_hg, batch),
        in_specs=[pl.BlockSpec(blk, lambda hg, b: (b, hg)),
                  pl.BlockSpec(blk, lambda hg, b: (b, n_hg + hg)),
                  pl.BlockSpec(blk, lambda hg, b: (b, 2 * n_hg + hg)),
                  pl.BlockSpec((1,) + bias_tbl.shape[1:], lambda hg, b: (hg, 0, 0, 0))],
        out_specs=pl.BlockSpec(blk, lambda hg, b: (b, hg)),
        out_shape=jax.ShapeDtypeStruct((batch * seq, D_MODEL), _BF16),
        compiler_params=_params("parallel", "parallel"),
        name="nbr_attention",
    )(qkv, qkv, qkv, bias_tbl)


POOL_TM = 512
POOL_HALO = 8


def _pool_kernel(tiles_per_seq, x_ref, prev_ref, next_ref, g_ref, w_ref, scale_ref, o_ref, hext_ref):
    i = pl.program_id(0)
    ti = i % tiles_per_seq
    seq = tiles_per_seq * POOL_TM
    g = g_ref[...]
    x = x_ref[...]
    h = _rms(x, g)
    has_prev = (ti > 0).astype(_F32)
    has_next = (ti < tiles_per_seq - 1).astype(_F32)
    hext_ref[pl.ds(0, POOL_HALO), :] = _rms(prev_ref[...], g) * has_prev
    hext_ref[pl.ds(POOL_HALO, POOL_TM), :] = h
    hext_ref[pl.ds(POOL_HALO + POOL_TM, POOL_HALO), :] = _rms(next_ref[...], g) * has_next
    t = ti * POOL_TM + lax.broadcasted_iota(jnp.int32, (POOL_TM, 1), 0)
    for gi, w in enumerate(POOL_WINDOWS):
        cols = slice(gi * GROUP_CH, (gi + 1) * GROUP_CH)
        lo = jnp.clip(t - w // 2, 0, seq)
        hi = jnp.clip(t + w - w // 2, 0, seq)
        cnt = (hi - lo).astype(_F32)
        acc = hext_ref[pl.ds(POOL_HALO - w // 2, POOL_TM), cols]
        for d in range(1 - w // 2, w - w // 2):
            acc = acc + hext_ref[pl.ds(POOL_HALO + d, POOL_TM), cols]
        mixed = (acc / cnt - h[:, cols]).astype(_BF16)
        out = _dot(mixed, w_ref[gi]) * scale_ref[:, cols]
        o_ref[:, cols] = x[:, cols] + out


def multiscale_pool(x, g, w_pool, scale, seq):
    m, d = x.shape
    tiles_per_seq = seq // POOL_TM
    halo_blocks = POOL_TM // POOL_HALO
    last_halo = m // POOL_HALO - 1
    return pl.pallas_call(
        functools.partial(_pool_kernel, tiles_per_seq),
        grid=(m // POOL_TM,),
        in_specs=[pl.BlockSpec((POOL_TM, d), lambda i: (i, 0)),
                  pl.BlockSpec((POOL_HALO, d), lambda i: (jnp.maximum(i * halo_blocks - 1, 0), 0)),
                  pl.BlockSpec((POOL_HALO, d),
                               lambda i: (jnp.minimum((i + 1) * halo_blocks, last_halo), 0)),
                  pl.BlockSpec((1, d), lambda i: (0, 0)),
                  pl.BlockSpec(w_pool.shape, lambda i: (0, 0, 0)),
                  pl.BlockSpec((1, d), lambda i: (0, 0))],
        out_specs=pl.BlockSpec((POOL_TM, d), lambda i: (i, 0)),
        out_shape=jax.ShapeDtypeStruct((m, d), _F32),
        scratch_shapes=[pltpu.VMEM((POOL_TM + 2 * POOL_HALO, d), _F32)],
        compiler_params=_params("parallel"),
        name="multiscale_pool",
    )(x, x, x, g.reshape(1, d), w_pool, scale.reshape(1, d))


def kernel(x, p, attn_norm_g, w_qkv, b_qkv, w_o, rpb, pool_norm_g, w_pool, pool_scale,
           ffn_norm_g, w_gate, w_up, w_down, ple_norm_g, w_ple_gate, b_ple_gate,
           w_ple_proj, final_norm_g):
    batch, seq, d = x.shape
    depth = p.shape[0]
    rows = seq // GRID_W
    assert d == D_MODEL and seq % GRID_W == 0 and rows >= WIN_H and seq % POOL_TM == 0
    m = batch * seq
    x = x.reshape(m, d)
    p = p.reshape(depth, m, p.shape[-1])
    q_scale = jnp.concatenate([jnp.full((d,), HEAD_DIM ** -0.5, _F32), jnp.ones((2 * d,), _F32)])

    for i in range(depth):
        j = i // 2
        if i % 2 == 0:
            xn = rmsnorm(x, attn_norm_g[j], _BF16)
            qkv = mm_bias_scale(xn, w_qkv[j].astype(_BF16), b_qkv[j], q_scale)
            o = neighbourhood_attention(qkv, _attention_bias_table(rpb[j], rows), batch, seq)
            x = mm_residual(o, w_o[j].astype(_BF16), x)
        else:
            x = multiscale_pool(x, pool_norm_g[j], w_pool[j].astype(_BF16), pool_scale[j], seq)
        xn = rmsnorm(x, ffn_norm_g[i], _BF16)
        hidden = mm_swiglu(xn, w_gate[i].astype(_BF16), w_up[i].astype(_BF16))
        x = mm_residual(hidden, w_down[i].astype(_BF16), x)
        xn = rmsnorm(x, ple_norm_g[i], _BF16)
        x = mm_ple(xn, w_ple_gate[i].astype(_BF16), b_ple_gate[i], p[i],
                   w_ple_proj[i].astype(_BF16), x)
    return rmsnorm(x, final_norm_g, _F32).reshape(batch, seq, d)
```

```python
import functools

import jax
import jax.numpy as jnp
import numpy as np
from jax import lax
from jax.experimental import pallas as pl
from jax.experimental.pallas import tpu as pltpu

D_MODEL = 2048
GRID_W = 64
N_HEADS = 64
HEAD_DIM = D_MODEL // N_HEADS
WIN_H = 8
WIN_W = 16
POOL_WINDOWS = (2, 4, 8, 16)
GROUP_CH = D_MODEL // len(POOL_WINDOWS)
RMS_EPS = 1e-6

VMEM_LIMIT_BYTES = 56 * 1024 * 1024
LANES = 128

HEADS_PER_GROUP = 8
GROUP_LANES = HEADS_PER_GROUP * HEAD_DIM
KEY_COLS = 32
Q_GROUPS = ((0, 24, 0), (24, 16, 16), (40, 24, 32))
for _q0, _nq, _k0 in Q_GROUPS:
    _cs = np.clip(np.arange(_q0, _q0 + _nq) - WIN_W // 2, 0, GRID_W - WIN_W)
    assert _cs.min() >= _k0 and _cs.max() + WIN_W <= _k0 + KEY_COLS
MASK_BIAS = -1e30

_BF16 = jnp.bfloat16
_F32 = jnp.float32


def _params(*semantics):
    return pltpu.CompilerParams(dimension_semantics=semantics,
                                vmem_limit_bytes=VMEM_LIMIT_BYTES)


def _dot(a, b):
    return jnp.dot(a, b, preferred_element_type=_F32)


def _row_scale(ssq_ref, d):
    total = ssq_ref[:, 0:LANES]
    for part in range(1, ssq_ref.shape[1] // LANES):
        total = total + ssq_ref[:, part * LANES:(part + 1) * LANES]
    return lax.rsqrt(total[:, 0:1] * (1.0 / d) + RMS_EPS)


def _emit_norm_inputs(x, g_ref, xg_ref, ssq_ref):
    xg_ref[...] = (x * g_ref[...]).astype(xg_ref.dtype)
    ssq_ref[...] = jnp.broadcast_to(jnp.sum(x * x, axis=-1, keepdims=True), ssq_ref.shape)


def _prep_kernel(x_ref, g_ref, xg_ref, ssq_ref):
    _emit_norm_inputs(x_ref[...], g_ref, xg_ref, ssq_ref)


def norm_inputs(x, g, tm=512):
    m, d = x.shape
    return pl.pallas_call(
        _prep_kernel,
        grid=(m // tm,),
        in_specs=[pl.BlockSpec((tm, d), lambda i: (i, 0)),
                  pl.BlockSpec((1, d), lambda i: (0, 0))],
        out_specs=[pl.BlockSpec((tm, d), lambda i: (i, 0)),
                   pl.BlockSpec((tm, LANES), lambda i: (i, 0))],
        out_shape=[jax.ShapeDtypeStruct((m, d), _BF16),
                   jax.ShapeDtypeStruct((m, LANES), _F32)],
        compiler_params=_params("parallel"),
        name="norm_inputs",
    )(x, g.reshape(1, d))


def _rmsnorm_kernel(x_ref, g_ref, o_ref):
    x = x_ref[...]
    o_ref[...] = x * lax.rsqrt(jnp.mean(x * x, axis=-1, keepdims=True) + RMS_EPS) * g_ref[...]


def rmsnorm(x, g, tm=512):
    m, d = x.shape
    return pl.pallas_call(
        _rmsnorm_kernel,
        grid=(m // tm,),
        in_specs=[pl.BlockSpec((tm, d), lambda i: (i, 0)),
                  pl.BlockSpec((1, d), lambda i: (0, 0))],
        out_specs=pl.BlockSpec((tm, d), lambda i: (i, 0)),
        out_shape=jax.ShapeDtypeStruct((m, d), _F32),
        compiler_params=_params("parallel"),
        name="rmsnorm",
    )(x, g.reshape(1, d))


CAST_ROWS = 256


def _cast_weight(w_ref, wb_ref):
    @pl.when(pl.program_id(1) == 0)
    def _():
        def body(c, carry):
            rows = pl.ds(pl.multiple_of(c * CAST_ROWS, CAST_ROWS), CAST_ROWS)
            wb_ref[rows, :] = w_ref[rows, :].astype(wb_ref.dtype)
            return carry
        lax.fori_loop(0, w_ref.shape[0] // CAST_ROWS, body, 0)


def _mm_qkv_kernel(d, xg_ref, ssq_ref, w_ref, b_ref, s_ref, o_ref, wb_ref):
    _cast_weight(w_ref, wb_ref)
    acc = _dot(xg_ref[...], wb_ref[...]) * _row_scale(ssq_ref, d)
    o_ref[...] = ((acc + b_ref[...]) * s_ref[...]).astype(o_ref.dtype)


def _mm_swiglu_kernel(d, xg_ref, ssq_ref, wg_ref, wu_ref, o_ref, wgb_ref, wub_ref):
    _cast_weight(wg_ref, wgb_ref)
    _cast_weight(wu_ref, wub_ref)
    a = xg_ref[...]
    r = _row_scale(ssq_ref, d)
    gate = _dot(a, wgb_ref[...]) * r
    up = _dot(a, wub_ref[...]) * r
    o_ref[...] = (gate * jax.nn.sigmoid(gate) * up).astype(o_ref.dtype)


def _mm_residual_kernel(a_ref, w_ref, res_ref, g_ref, x_ref, xg_ref, ssq_ref, wb_ref):
    _cast_weight(w_ref, wb_ref)
    x = res_ref[...] + _dot(a_ref[...], wb_ref[...])
    x_ref[...] = x
    _emit_norm_inputs(x, g_ref, xg_ref, ssq_ref)


def _mm_ple_kernel(d, emit, xg_ref, ssq_ref, wg_ref, b_ref, p_ref, wp_ref, res_ref, *rest):
    if emit:
        g_ref, x_ref, xg_out_ref, ssq_out_ref, wgb_ref, wpb_ref = rest
    else:
        x_ref, wgb_ref, wpb_ref = rest
    _cast_weight(wg_ref, wgb_ref)
    _cast_weight(wp_ref, wpb_ref)
    gate = jax.nn.sigmoid(_dot(xg_ref[...], wgb_ref[...]) * _row_scale(ssq_ref, d) + b_ref[...])
    proj = _dot(p_ref[...].astype(_BF16), wpb_ref[...])
    x = res_ref[...] + proj * gate
    x_ref[...] = x
    if emit:
        _emit_norm_inputs(x, g_ref, xg_out_ref, ssq_out_ref)


def _row_spec(tm, k):
    return pl.BlockSpec((tm, k), lambda j, i: (i, 0))


def _col_spec(k, tn):
    return pl.BlockSpec((k, tn), lambda j, i: (0, j))


def _vec_spec(tn):
    return pl.BlockSpec((1, tn), lambda j, i: (0, j))


def _tile_spec(tm, tn):
    return pl.BlockSpec((tm, tn), lambda j, i: (i, j))


def _ssq_out_spec(tm):
    return pl.BlockSpec((tm, LANES), lambda j, i: (i, j))


def _mm_params():
    return _params("parallel", "arbitrary")


def mm_qkv(xg, ssq, w, b, s, tm=512, tn=1024):
    m, k = xg.shape
    n = w.shape[1]
    return pl.pallas_call(
        functools.partial(_mm_qkv_kernel, k),
        grid=(n // tn, m // tm),
        in_specs=[_row_spec(tm, k), _row_spec(tm, ssq.shape[1]), _col_spec(k, tn),
                  _vec_spec(tn), _vec_spec(tn)],
        out_specs=_tile_spec(tm, tn),
        out_shape=jax.ShapeDtypeStruct((m, n), _BF16),
        scratch_shapes=[pltpu.VMEM((k, tn), _BF16)],
        compiler_params=_mm_params(),
        name="mm_qkv",
    )(xg, ssq, w, b.reshape(1, n), s.reshape(1, n))


def mm_swiglu(xg, ssq, wg, wu, tm=512, tn=512):
    m, k = xg.shape
    n = wg.shape[1]
    return pl.pallas_call(
        functools.partial(_mm_swiglu_kernel, k),
        grid=(n // tn, m // tm),
        in_specs=[_row_spec(tm, k), _row_spec(tm, ssq.shape[1]), _col_spec(k, tn), _col_spec(k, tn)],
        out_specs=_tile_spec(tm, tn),
        out_shape=jax.ShapeDtypeStruct((m, n), _BF16),
        scratch_shapes=[pltpu.VMEM((k, tn), _BF16), pltpu.VMEM((k, tn), _BF16)],
        compiler_params=_mm_params(),
        name="mm_swiglu",
    )(xg, ssq, wg, wu)


def _norm_out(m, n, tm, tn):
    specs = [_tile_spec(tm, tn), _tile_spec(tm, tn), _ssq_out_spec(tm)]
    shapes = [jax.ShapeDtypeStruct((m, n), _F32), jax.ShapeDtypeStruct((m, n), _BF16),
              jax.ShapeDtypeStruct((m, LANES * (n // tn)), _F32)]
    return specs, shapes


def mm_residual(a, w, res, g_next, tm=512, tn=512):
    m, k = a.shape
    n = w.shape[1]
    out_specs, out_shape = _norm_out(m, n, tm, tn)
    return pl.pallas_call(
        _mm_residual_kernel,
        grid=(n // tn, m // tm),
        in_specs=[_row_spec(tm, k), _col_spec(k, tn), _tile_spec(tm, tn), _vec_spec(tn)],
        out_specs=out_specs,
        out_shape=out_shape,
        scratch_shapes=[pltpu.VMEM((k, tn), _BF16)],
        compiler_params=_mm_params(),
        name="mm_residual",
    )(a, w, res, g_next.reshape(1, n))


def mm_ple(xg, ssq, wg, b, p, layer, wp, res, g_next, tm=512, tn=1024):
    m, k = xg.shape
    n = wg.shape[1]
    kp = p.shape[-1]
    emit = g_next is not None
    in_specs = [_row_spec(tm, k), _row_spec(tm, ssq.shape[1]), _col_spec(k, tn), _vec_spec(tn),
                pl.BlockSpec((None, tm, kp), lambda j, i: (layer, i, 0)), _col_spec(kp, tn),
                _tile_spec(tm, tn)]
    args = [xg, ssq, wg, b.reshape(1, n), p, wp, res]
    if emit:
        in_specs.append(_vec_spec(tn))
        args.append(g_next.reshape(1, n))
        out_specs, out_shape = _norm_out(m, n, tm, tn)
    else:
        out_specs, out_shape = _tile_spec(tm, tn), jax.ShapeDtypeStruct((m, n), _F32)
    return pl.pallas_call(
        functools.partial(_mm_ple_kernel, k, emit),
        grid=(n // tn, m // tm),
        in_specs=in_specs,
        out_specs=out_specs,
        out_shape=out_shape,
        scratch_shapes=[pltpu.VMEM((k, tn), _BF16), pltpu.VMEM((kp, tn), _BF16)],
        compiler_params=_mm_params(),
        name="mm_ple",
    )(*args)


def _edge_case_offsets(rows):
    half = WIN_H // 2
    edge_rows = list(range(half + 1)) + list(range(rows - half + 1, rows))
    return [int(np.clip(r - half, 0, rows - WIN_H)) - r for r in edge_rows]


BIAS_ROW_CHUNK = 64


def _attn_kernel(q_ref, k_ref, v_ref, u_ref, o_ref, bias_ref):
    rows = q_ref.shape[0] // GRID_W
    lane_head = lax.broadcasted_iota(jnp.int32, (1, GROUP_LANES), 1) // HEAD_DIM
    head_masks = [lane_head == h for h in range(HEADS_PER_GROUP)]

    @pl.when(pl.program_id(1) == 0)
    def _():
        for c, off in enumerate(_edge_case_offsets(rows)):
            for r0 in range(0, bias_ref.shape[1], BIAS_ROW_CHUNK):
                bias_ref[c, pl.ds(r0, BIAS_ROW_CHUNK), :] = jnp.concatenate(
                    [u_ref[0, off + WIN_H - 1 + j, pl.ds(r0, BIAS_ROW_CHUNK), :]
                     for j in range(WIN_H)], axis=1)

    def row_body(r, carry):
        r_start = jnp.clip(r - WIN_H // 2, 0, rows - WIN_H)
        case = jnp.minimum(r, WIN_H // 2) + jnp.maximum(r - (rows - WIN_H // 2), 0)
        row0 = pl.multiple_of(r * GRID_W, GRID_W)
        q_row = q_ref[pl.ds(row0, GRID_W), :].astype(_F32)
        scores = []
        bias_row = 0
        for q0, nq, k0 in Q_GROUPS:
            qg = q_row[q0:q0 + nq]
            qm = jnp.concatenate([jnp.where(mask, qg, 0.0) for mask in head_masks],
                                 axis=0).astype(_BF16)
            starts = [pl.multiple_of((r_start + j) * GRID_W + k0, 16) for j in range(WIN_H)]
            kwin = jnp.concatenate([k_ref[pl.ds(s, KEY_COLS), :] for s in starts], axis=0)
            s = lax.dot_general(qm, kwin, (((1,), (1,)), ((), ())),
                                preferred_element_type=_F32)
            scores.append(s + bias_ref[case, pl.ds(bias_row, HEADS_PER_GROUP * nq), :])
            bias_row += HEADS_PER_GROUP * nq
        exps = [jnp.exp(s - jnp.max(s, axis=-1, keepdims=True)) for s in scores]
        outs = []
        for (q0, nq, k0), e in zip(Q_GROUPS, exps):
            starts = [pl.multiple_of((r_start + j) * GRID_W + k0, 16) for j in range(WIN_H)]
            vwin = jnp.concatenate([v_ref[pl.ds(s, KEY_COLS), :] for s in starts], axis=0)
            o_all = _dot(e.astype(_BF16), vwin)
            o_all = o_all / jnp.sum(e, axis=-1, keepdims=True)
            o = jnp.where(head_masks[0], o_all[:nq], 0.0)
            for h in range(1, HEADS_PER_GROUP):
                o = o + jnp.where(head_masks[h], o_all[h * nq:(h + 1) * nq], 0.0)
            outs.append(o)
        o_ref[pl.ds(row0, GRID_W), :] = jnp.concatenate(outs, axis=0).astype(o_ref.dtype)
        return carry

    lax.fori_loop(0, rows, row_body, 0, unroll=2)


def _attention_bias_rows(rpb):
    n_hg = N_HEADS // HEADS_PER_GROUP
    pad = KEY_COLS // 2
    ext = jnp.pad(rpb, ((0, 0), (0, 0), (pad, pad)), mode="edge")
    parts = []
    for q0, nq, k0 in Q_GROUPS:
        qc = q0 + np.arange(nq)[:, None]
        kc = k0 + np.arange(KEY_COLS)[None, :]
        c_start = np.clip(qc - WIN_W // 2, 0, GRID_W - WIN_W)
        valid = (kc >= c_start) & (kc < c_start + WIN_W)
        first = [k0 - (q0 + i) + (WIN_W - 1) + pad for i in range(nq)]
        assert min(first) >= 0 and max(first) + KEY_COLS <= ext.shape[-1]
        part = jnp.stack([ext[:, :, f:f + KEY_COLS] for f in first], axis=2)
        part = jnp.where(valid[None, None], part, MASK_BIAS)
        part = part.reshape(n_hg, HEADS_PER_GROUP, 2 * WIN_H - 1, nq, KEY_COLS)
        part = jnp.transpose(part, (0, 2, 1, 3, 4))
        parts.append(part.reshape(n_hg, 2 * WIN_H - 1, HEADS_PER_GROUP * nq, KEY_COLS))
    return jnp.concatenate(parts, axis=2)


def neighbourhood_attention(qkv, bias_rows, batch, seq):
    n_hg = N_HEADS // HEADS_PER_GROUP
    blk = (seq, GROUP_LANES)
    n_cases = len(_edge_case_offsets(seq // GRID_W))
    return pl.pallas_call(
        _attn_kernel,
        grid=(n_hg, batch),
        in_specs=[pl.BlockSpec(blk, lambda hg, b: (b, hg)),
                  pl.BlockSpec(blk, lambda hg, b: (b, n_hg + hg)),
                  pl.BlockSpec(blk, lambda hg, b: (b, 2 * n_hg + hg)),
                  pl.BlockSpec((1,) + bias_rows.shape[1:], lambda hg, b: (hg, 0, 0, 0))],
        out_specs=pl.BlockSpec(blk, lambda hg, b: (b, hg)),
        out_shape=jax.ShapeDtypeStruct((batch * seq, D_MODEL), _BF16),
        scratch_shapes=[pltpu.VMEM((n_cases, HEADS_PER_GROUP * GRID_W, WIN_H * KEY_COLS), _F32)],
        compiler_params=_params("arbitrary", "arbitrary"),
        name="nbr_attention",
    )(qkv, qkv, qkv, bias_rows)


POOL_TM = 512
POOL_HALO = 8


def _pool_kernel(tiles_per_seq, x_ref, prev_ref, next_ref, g_ref, w_ref, scale_ref, gn_ref,
                 o_ref, xg_ref, ssq_ref, hext_ref, wb_ref):
    i = pl.program_id(0)

    @pl.when(i == 0)
    def _():
        wb_ref[...] = w_ref[...].astype(wb_ref.dtype)

    def rms(v):
        return v * lax.rsqrt(jnp.mean(v * v, axis=-1, keepdims=True) + RMS_EPS) * g_ref[...]

    ti = i % tiles_per_seq
    seq = tiles_per_seq * POOL_TM
    x = x_ref[...]
    h = rms(x)
    has_prev = (ti > 0).astype(_F32)
    has_next = (ti < tiles_per_seq - 1).astype(_F32)
    hext_ref[pl.ds(0, POOL_HALO), :] = rms(prev_ref[...]) * has_prev
    hext_ref[pl.ds(POOL_HALO, POOL_TM), :] = h
    hext_ref[pl.ds(POOL_HALO + POOL_TM, POOL_HALO), :] = rms(next_ref[...]) * has_next
    t = ti * POOL_TM + lax.broadcasted_iota(jnp.int32, (POOL_TM, 1), 0)
    outs = []
    for gi, w in enumerate(POOL_WINDOWS):
        cols = slice(gi * GROUP_CH, (gi + 1) * GROUP_CH)
        lo = jnp.clip(t - w // 2, 0, seq)
        hi = jnp.clip(t + w - w // 2, 0, seq)
        cnt = (hi - lo).astype(_F32)
        acc = hext_ref[pl.ds(POOL_HALO - w // 2, POOL_TM), cols]
        for d in range(1 - w // 2, w - w // 2):
            acc = acc + hext_ref[pl.ds(POOL_HALO + d, POOL_TM), cols]
        mixed = (acc / cnt - h[:, cols]).astype(_BF16)
        outs.append(x[:, cols] + _dot(mixed, wb_ref[gi]) * scale_ref[:, cols])
    x_new = jnp.concatenate(outs, axis=1)
    o_ref[...] = x_new
    _emit_norm_inputs(x_new, gn_ref, xg_ref, ssq_ref)


def multiscale_pool(x, g, w_pool, scale, g_next, seq):
    m, d = x.shape
    tiles_per_seq = seq // POOL_TM
    halo_blocks = POOL_TM // POOL_HALO
    last_halo = m // POOL_HALO - 1
    vec = pl.BlockSpec((1, d), lambda i: (0, 0))
    return pl.pallas_call(
        functools.partial(_pool_kernel, tiles_per_seq),
        grid=(m // POOL_TM,),
        in_specs=[pl.BlockSpec((POOL_TM, d), lambda i: (i, 0)),
                  pl.BlockSpec((POOL_HALO, d), lambda i: (jnp.maximum(i * halo_blocks - 1, 0), 0)),
                  pl.BlockSpec((POOL_HALO, d),
                               lambda i: (jnp.minimum((i + 1) * halo_blocks, last_halo), 0)),
                  vec,
                  pl.BlockSpec(w_pool.shape, lambda i: (0, 0, 0)),
                  vec, vec],
        out_specs=[pl.BlockSpec((POOL_TM, d), lambda i: (i, 0)),
                   pl.BlockSpec((POOL_TM, d), lambda i: (i, 0)),
                   pl.BlockSpec((POOL_TM, LANES), lambda i: (i, 0))],
        out_shape=[jax.ShapeDtypeStruct((m, d), _F32), jax.ShapeDtypeStruct((m, d), _BF16),
                   jax.ShapeDtypeStruct((m, LANES), _F32)],
        scratch_shapes=[pltpu.VMEM((POOL_TM + 2 * POOL_HALO, d), _F32),
                        pltpu.VMEM(w_pool.shape, _BF16)],
        compiler_params=_params("arbitrary"),
        name="multiscale_pool",
    )(x, x, x, g.reshape(1, d), w_pool, scale.reshape(1, d), g_next.reshape(1, d))


def kernel(x, p, attn_norm_g, w_qkv, b_qkv, w_o, rpb, pool_norm_g, w_pool, pool_scale,
           ffn_norm_g, w_gate, w_up, w_down, ple_norm_g, w_ple_gate, b_ple_gate,
           w_ple_proj, final_norm_g):
    batch, seq, d = x.shape
    depth = p.shape[0]
    rows = seq // GRID_W
    assert d == D_MODEL and seq % GRID_W == 0 and rows >= WIN_H and seq % POOL_TM == 0
    m = batch * seq
    x = x.reshape(m, d)
    p = p.reshape(depth, m, p.shape[-1])
    q_scale = jnp.concatenate([jnp.full((d,), HEAD_DIM ** -0.5, _F32), jnp.ones((2 * d,), _F32)])

    xg, ssq = norm_inputs(x, attn_norm_g[0])
    for i in range(depth):
        j = i // 2
        if i % 2 == 0:
            qkv = mm_qkv(xg, ssq, w_qkv[j], b_qkv[j], q_scale)
            o = neighbourhood_attention(qkv, _attention_bias_rows(rpb[j]), batch, seq)
            x, xg, ssq = mm_residual(o, w_o[j], x, ffn_norm_g[i], tn=1024)
        else:
            x, xg, ssq = multiscale_pool(x, pool_norm_g[j], w_pool[j], pool_scale[j],
                                         ffn_norm_g[i], seq)
        hidden = mm_swiglu(xg, ssq, w_gate[i], w_up[i])
        x, xg, ssq = mm_residual(hidden, w_down[i], x, ple_norm_g[i])
        if i + 1 < depth and (i + 1) % 2 == 0:
            x, xg, ssq = mm_ple(xg, ssq, w_ple_gate[i], b_ple_gate[i], p, i, w_ple_proj[i], x,
                                attn_norm_g[(i + 1) // 2])
        else:
            x = mm_ple(xg, ssq, w_ple_gate[i], b_ple_gate[i], p, i, w_ple_proj[i], x, None)
    return rmsnorm(x, final_norm_g).reshape(batch, seq, d)
```

```python
import functools

import jax
import jax.numpy as jnp
import numpy as np
from jax import lax
from jax.experimental import pallas as pl
from jax.experimental.pallas import tpu as pltpu

D_MODEL = 2048
GRID_W = 64
N_HEADS = 64
HEAD_DIM = D_MODEL // N_HEADS
WIN_H = 8
WIN_W = 16
POOL_WINDOWS = (2, 4, 8, 16)
GROUP_CH = D_MODEL // len(POOL_WINDOWS)
RMS_EPS = 1e-6

VMEM_LIMIT_BYTES = 56 * 1024 * 1024
LANES = 128

HEADS_PER_GROUP = 8
GROUP_LANES = HEADS_PER_GROUP * HEAD_DIM
KEY_COLS = 32
Q_GROUPS = ((0, 24, 0), (24, 16, 16), (40, 24, 32))
for _q0, _nq, _k0 in Q_GROUPS:
    _cs = np.clip(np.arange(_q0, _q0 + _nq) - WIN_W // 2, 0, GRID_W - WIN_W)
    assert _cs.min() >= _k0 and _cs.max() + WIN_W <= _k0 + KEY_COLS
MASK_BIAS = -1e30

_BF16 = jnp.bfloat16
_F32 = jnp.float32


def _params(*semantics):
    return pltpu.CompilerParams(dimension_semantics=semantics,
                                vmem_limit_bytes=VMEM_LIMIT_BYTES)


def _dot(a, b):
    return jnp.dot(a, b, preferred_element_type=_F32)


def _row_scale(ssq_ref, d):
    total = ssq_ref[:, 0:LANES]
    for part in range(1, ssq_ref.shape[1] // LANES):
        total = total + ssq_ref[:, part * LANES:(part + 1) * LANES]
    return lax.rsqrt(total[:, 0:1] * (1.0 / d) + RMS_EPS)


def _emit_norm_inputs(x, g_ref, xg_ref, ssq_ref):
    xg_ref[...] = (x * g_ref[...]).astype(xg_ref.dtype)
    ssq_ref[...] = jnp.broadcast_to(jnp.sum(x * x, axis=-1, keepdims=True), ssq_ref.shape)


def _prep_kernel(x_ref, g_ref, xg_ref, ssq_ref):
    _emit_norm_inputs(x_ref[...], g_ref, xg_ref, ssq_ref)


def norm_inputs(x, g, tm=512):
    m, d = x.shape
    return pl.pallas_call(
        _prep_kernel,
        grid=(m // tm,),
        in_specs=[pl.BlockSpec((tm, d), lambda i: (i, 0)),
                  pl.BlockSpec((1, d), lambda i: (0, 0))],
        out_specs=[pl.BlockSpec((tm, d), lambda i: (i, 0)),
                   pl.BlockSpec((tm, LANES), lambda i: (i, 0))],
        out_shape=[jax.ShapeDtypeStruct((m, d), _BF16),
                   jax.ShapeDtypeStruct((m, LANES), _F32)],
        compiler_params=_params("parallel"),
        name="norm_inputs",
    )(x, g.reshape(1, d))


def _rmsnorm_kernel(x_ref, g_ref, o_ref):
    x = x_ref[...]
    o_ref[...] = x * lax.rsqrt(jnp.mean(x * x, axis=-1, keepdims=True) + RMS_EPS) * g_ref[...]


def rmsnorm(x, g, tm=512):
    m, d = x.shape
    return pl.pallas_call(
        _rmsnorm_kernel,
        grid=(m // tm,),
        in_specs=[pl.BlockSpec((tm, d), lambda i: (i, 0)),
                  pl.BlockSpec((1, d), lambda i: (0, 0))],
        out_specs=pl.BlockSpec((tm, d), lambda i: (i, 0)),
        out_shape=jax.ShapeDtypeStruct((m, d), _F32),
        compiler_params=_params("parallel"),
        name="rmsnorm",
    )(x, g.reshape(1, d))


CAST_ROWS = 256


def _cast_weight(w_ref, wb_ref):
    @pl.when(pl.program_id(1) == 0)
    def _():
        def body(c, carry):
            rows = pl.ds(pl.multiple_of(c * CAST_ROWS, CAST_ROWS), CAST_ROWS)
            wb_ref[rows, :] = w_ref[rows, :].astype(wb_ref.dtype)
            return carry
        lax.fori_loop(0, w_ref.shape[0] // CAST_ROWS, body, 0)


def _mm_qkv_kernel(d, xg_ref, ssq_ref, w_ref, b_ref, s_ref, o_ref, wb_ref):
    _cast_weight(w_ref, wb_ref)
    acc = _dot(xg_ref[...], wb_ref[...]) * _row_scale(ssq_ref, d)
    o_ref[...] = ((acc + b_ref[...]) * s_ref[...]).astype(o_ref.dtype)


def _mm_swiglu_kernel(d, xg_ref, ssq_ref, wg_ref, wu_ref, o_ref, wgb_ref, wub_ref):
    _cast_weight(wg_ref, wgb_ref)
    _cast_weight(wu_ref, wub_ref)
    a = xg_ref[...]
    r = _row_scale(ssq_ref, d)
    half = o_ref.shape[1] // 2
    for c in range(2):
        cols = slice(c * half, (c + 1) * half)
        gate = _dot(a, wgb_ref[:, cols]) * r
        up = _dot(a, wub_ref[:, cols]) * r
        o_ref[:, cols] = (gate * jax.nn.sigmoid(gate) * up).astype(o_ref.dtype)


def _mm_residual_kernel(a_ref, w_ref, res_ref, g_ref, x_ref, xg_ref, ssq_ref, wb_ref):
    _cast_weight(w_ref, wb_ref)
    x = res_ref[...] + _dot(a_ref[...], wb_ref[...])
    x_ref[...] = x
    _emit_norm_inputs(x, g_ref, xg_ref, ssq_ref)


def _mm_ple_kernel(d, emit, xg_ref, ssq_ref, wg_ref, b_ref, p_ref, wp_ref, res_ref, *rest):
    if emit:
        g_ref, x_ref, xg_out_ref, ssq_out_ref, wgb_ref, wpb_ref = rest
    else:
        x_ref, wgb_ref, wpb_ref = rest
    _cast_weight(wg_ref, wgb_ref)
    _cast_weight(wp_ref, wpb_ref)
    gate = jax.nn.sigmoid(_dot(xg_ref[...], wgb_ref[...]) * _row_scale(ssq_ref, d) + b_ref[...])
    proj = _dot(p_ref[...].astype(_BF16), wpb_ref[...])
    x = res_ref[...] + proj * gate
    x_ref[...] = x
    if emit:
        _emit_norm_inputs(x, g_ref, xg_out_ref, ssq_out_ref)


def _row_spec(tm, k):
    return pl.BlockSpec((tm, k), lambda j, i: (i, 0))


def _col_spec(k, tn):
    return pl.BlockSpec((k, tn), lambda j, i: (0, j))


def _vec_spec(tn):
    return pl.BlockSpec((1, tn), lambda j, i: (0, j))


def _tile_spec(tm, tn):
    return pl.BlockSpec((tm, tn), lambda j, i: (i, j))


def _ssq_out_spec(tm):
    return pl.BlockSpec((tm, LANES), lambda j, i: (i, j))


def _mm_params():
    return _params("parallel", "arbitrary")


def mm_qkv(xg, ssq, w, b, s, tm=1024, tn=1024):
    m, k = xg.shape
    n = w.shape[1]
    return pl.pallas_call(
        functools.partial(_mm_qkv_kernel, k),
        grid=(n // tn, m // tm),
        in_specs=[_row_spec(tm, k), _row_spec(tm, ssq.shape[1]), _col_spec(k, tn),
                  _vec_spec(tn), _vec_spec(tn)],
        out_specs=_tile_spec(tm, tn),
        out_shape=jax.ShapeDtypeStruct((m, n), _BF16),
        scratch_shapes=[pltpu.VMEM((k, tn), _BF16)],
        compiler_params=_mm_params(),
        name="mm_qkv",
    )(xg, ssq, w, b.reshape(1, n), s.reshape(1, n))


def mm_swiglu(xg, ssq, wg, wu, tm=1024, tn=512):
    m, k = xg.shape
    n = wg.shape[1]
    return pl.pallas_call(
        functools.partial(_mm_swiglu_kernel, k),
        grid=(n // tn, m // tm),
        in_specs=[_row_spec(tm, k), _row_spec(tm, ssq.shape[1]), _col_spec(k, tn), _col_spec(k, tn)],
        out_specs=_tile_spec(tm, tn),
        out_shape=jax.ShapeDtypeStruct((m, n), _BF16),
        scratch_shapes=[pltpu.VMEM((k, tn), _BF16), pltpu.VMEM((k, tn), _BF16)],
        compiler_params=_mm_params(),
        name="mm_swiglu",
    )(xg, ssq, wg, wu)


def _norm_out(m, n, tm, tn):
    specs = [_tile_spec(tm, tn), _tile_spec(tm, tn), _ssq_out_spec(tm)]
    shapes = [jax.ShapeDtypeStruct((m, n), _F32), jax.ShapeDtypeStruct((m, n), _BF16),
              jax.ShapeDtypeStruct((m, LANES * (n // tn)), _F32)]
    return specs, shapes


def mm_residual(a, w, res, g_next, tm=512, tn=512):
    m, k = a.shape
    n = w.shape[1]
    out_specs, out_shape = _norm_out(m, n, tm, tn)
    return pl.pallas_call(
        _mm_residual_kernel,
        grid=(n // tn, m // tm),
        in_specs=[_row_spec(tm, k), _col_spec(k, tn), _tile_spec(tm, tn), _vec_spec(tn)],
        out_specs=out_specs,
        out_shape=out_shape,
        scratch_shapes=[pltpu.VMEM((k, tn), _BF16)],
        compiler_params=_mm_params(),
        name="mm_residual",
    )(a, w, res, g_next.reshape(1, n))


def mm_ple(xg, ssq, wg, b, p, layer, wp, res, g_next, tm=512, tn=1024):
    m, k = xg.shape
    n = wg.shape[1]
    kp = p.shape[-1]
    emit = g_next is not None
    in_specs = [_row_spec(tm, k), _row_spec(tm, ssq.shape[1]), _col_spec(k, tn), _vec_spec(tn),
                pl.BlockSpec((None, tm, kp), lambda j, i: (layer, i, 0)), _col_spec(kp, tn),
                _tile_spec(tm, tn)]
    args = [xg, ssq, wg, b.reshape(1, n), p, wp, res]
    if emit:
        in_specs.append(_vec_spec(tn))
        args.append(g_next.reshape(1, n))
        out_specs, out_shape = _norm_out(m, n, tm, tn)
    else:
        out_specs, out_shape = _tile_spec(tm, tn), jax.ShapeDtypeStruct((m, n), _F32)
    return pl.pallas_call(
        functools.partial(_mm_ple_kernel, k, emit),
        grid=(n // tn, m // tm),
        in_specs=in_specs,
        out_specs=out_specs,
        out_shape=out_shape,
        scratch_shapes=[pltpu.VMEM((k, tn), _BF16), pltpu.VMEM((kp, tn), _BF16)],
        compiler_params=_mm_params(),
        name="mm_ple",
    )(*args)


EXT_PAD = KEY_COLS // 2
N_BIAS_ROWS = 2 * WIN_H - 1
N_QUADS = N_BIAS_ROWS - LANES // KEY_COLS + 1
GROUP_ROWS = HEADS_PER_GROUP * GRID_W


def _build_bias_quads(ext_ref, quad_ref):
    lane = lax.broadcasted_iota(jnp.int32, (GRID_W, LANES), 1)
    qc = lax.broadcasted_iota(jnp.int32, (GRID_W, LANES), 0)
    lane_j = lane // KEY_COLS
    k0 = jnp.where(qc < Q_GROUPS[1][0], Q_GROUPS[0][2],
                   jnp.where(qc < Q_GROUPS[2][0], Q_GROUPS[1][2], Q_GROUPS[2][2]))
    kc = k0 + lane % KEY_COLS
    c_start = jnp.clip(qc - WIN_W // 2, 0, GRID_W - WIN_W)
    valid = (kc >= c_start) & (kc < c_start + WIN_W)

    def head_body(h, carry):
        def rolled(dr, j):
            row = ext_ref[0, pl.ds(h * N_BIAS_ROWS + dr, 1), :]
            parts = []
            for q0, nq, k0_g in Q_GROUPS:
                shift = (KEY_COLS * j - k0_g - (WIN_W - 1) - EXT_PAD + q0) % LANES
                parts.append(pltpu.roll(jnp.broadcast_to(row, (nq, LANES)), shift, 1,
                                        stride=1, stride_axis=0))
            return jnp.concatenate(parts, axis=0)

        for d0 in range(N_QUADS):
            quad = rolled(d0, 0)
            for j in range(1, LANES // KEY_COLS):
                quad = jnp.where(lane_j == j, rolled(d0 + j, j), quad)
            quad_ref[d0, pl.ds(pl.multiple_of(h * GRID_W, GRID_W), GRID_W), :] = (
                jnp.where(valid, quad, MASK_BIAS))
        return carry

    lax.fori_loop(0, HEADS_PER_GROUP, head_body, 0)


def _attn_kernel(q_ref, k_ref, v_ref, ext_ref, o_ref, quad_ref,
                 s0_ref, s1_ref, e0_ref, e1_ref, l0_ref, l1_ref):
    rows = q_ref.shape[0] // GRID_W
    lane_head = lax.broadcasted_iota(jnp.int32, (1, GROUP_LANES), 1) // HEAD_DIM
    head_masks = [lane_head == h for h in range(HEADS_PER_GROUP)]

    @pl.when(pl.program_id(1) == 0)
    def _():
        _build_bias_quads(ext_ref, quad_ref)

    def window(ref, r, k0):
        r_start = jnp.clip(r - WIN_H // 2, 0, rows - WIN_H)
        starts = [pl.multiple_of((r_start + j) * GRID_W + k0, 16) for j in range(WIN_H)]
        return jnp.concatenate([ref[pl.ds(s, KEY_COLS), :] for s in starts], axis=0)

    def score_stage(r, s_ref):
        d0 = jnp.clip(r - WIN_H // 2, 0, rows - WIN_H) - r + (WIN_H - 1)
        q_row = q_ref[pl.ds(pl.multiple_of(r * GRID_W, GRID_W), GRID_W), :].astype(_F32)
        off = 0
        for q0, nq, k0 in Q_GROUPS:
            qg = q_row[q0:q0 + nq]
            qm = jnp.concatenate([jnp.where(mask, qg, 0.0) for mask in head_masks],
                                 axis=0).astype(_BF16)
            s = lax.dot_general(qm, window(k_ref, r, k0), (((1,), (1,)), ((), ())),
                                preferred_element_type=_F32)
            bias = jnp.concatenate(
                [jnp.concatenate([quad_ref[d0 + half * (LANES // KEY_COLS),
                                           pl.ds(h * GRID_W + q0, nq), :]
                                  for h in range(HEADS_PER_GROUP)], axis=0)
                 for half in range(2)], axis=1)
            s_ref[pl.ds(off, HEADS_PER_GROUP * nq), :] = s + bias
            off += HEADS_PER_GROUP * nq

    def exp_stage(s_ref, e_ref, l_ref):
        off = 0
        for _, nq, _ in Q_GROUPS:
            rows_g = pl.ds(off, HEADS_PER_GROUP * nq)
            s = s_ref[rows_g, :]
            e = jnp.exp(s - jnp.max(s, axis=-1, keepdims=True))
            e_ref[rows_g, :] = e.astype(e_ref.dtype)
            l_ref[rows_g, :] = jnp.broadcast_to(1.0 / jnp.sum(e, axis=-1, keepdims=True),
                                                (HEADS_PER_GROUP * nq, LANES))
            off += HEADS_PER_GROUP * nq

    def value_stage(r, e_ref, l_ref):
        outs = []
        off = 0
        for q0, nq, k0 in Q_GROUPS:
            rows_g = pl.ds(off, HEADS_PER_GROUP * nq)
            o_all = _dot(e_ref[rows_g, :], window(v_ref, r, k0))
            inv = l_ref[rows_g, :]
            o_all = o_all * jnp.concatenate([inv, inv], axis=1)
            o = jnp.where(head_masks[0], o_all[:nq], 0.0)
            for h in range(1, HEADS_PER_GROUP):
                o = o + jnp.where(head_masks[h], o_all[h * nq:(h + 1) * nq], 0.0)
            outs.append(o)
            off += HEADS_PER_GROUP * nq
        o_ref[pl.ds(pl.multiple_of(r * GRID_W, GRID_W), GRID_W), :] = (
            jnp.concatenate(outs, axis=0).astype(o_ref.dtype))

    score_stage(0, s0_ref)
    score_stage(1, s1_ref)
    exp_stage(s0_ref, e0_ref, l0_ref)

    def pair_body(k, carry):
        r = 2 * k
        score_stage(r, s0_ref)
        exp_stage(s1_ref, e1_ref, l1_ref)
        value_stage(r - 2, e0_ref, l0_ref)
        score_stage(r + 1, s1_ref)
        exp_stage(s0_ref, e0_ref, l0_ref)
        value_stage(r - 1, e1_ref, l1_ref)
        return carry

    lax.fori_loop(1, rows // 2, pair_body, 0)
    exp_stage(s1_ref, e1_ref, l1_ref)
    value_stage(rows - 2, e0_ref, l0_ref)
    value_stage(rows - 1, e1_ref, l1_ref)


def _attention_bias_ext(rpb):
    ext = jnp.pad(rpb, ((0, 0), (0, 0), (EXT_PAD, EXT_PAD)), mode="edge")
    ext = jnp.pad(ext, ((0, 0), (0, 0), (0, LANES - ext.shape[-1])))
    return ext.reshape(N_HEADS // HEADS_PER_GROUP, HEADS_PER_GROUP * N_BIAS_ROWS, LANES)


def neighbourhood_attention(qkv, rpb, batch, seq):
    n_hg = N_HEADS // HEADS_PER_GROUP
    rows = seq // GRID_W
    assert rows % 2 == 0 and rows >= max(WIN_H, 4)
    blk = (seq, GROUP_LANES)
    ext = _attention_bias_ext(rpb)
    score_scratch = pltpu.VMEM((GROUP_ROWS, WIN_H * KEY_COLS), _F32)
    exp_scratch = pltpu.VMEM((GROUP_ROWS, WIN_H * KEY_COLS), _BF16)
    inv_scratch = pltpu.VMEM((GROUP_ROWS, LANES), _F32)
    return pl.pallas_call(
        _attn_kernel,
        grid=(n_hg, batch),
        in_specs=[pl.BlockSpec(blk, lambda hg, b: (b, hg)),
                  pl.BlockSpec(blk, lambda hg, b: (b, n_hg + hg)),
                  pl.BlockSpec(blk, lambda hg, b: (b, 2 * n_hg + hg)),
                  pl.BlockSpec((1,) + ext.shape[1:], lambda hg, b: (hg, 0, 0))],
        out_specs=pl.BlockSpec(blk, lambda hg, b: (b, hg)),
        out_shape=jax.ShapeDtypeStruct((batch * seq, D_MODEL), _BF16),
        scratch_shapes=[pltpu.VMEM((N_QUADS, GROUP_ROWS, LANES), _F32),
                        score_scratch, score_scratch, exp_scratch, exp_scratch,
                        inv_scratch, inv_scratch],
        compiler_params=_params("arbitrary", "arbitrary"),
        name="nbr_attention",
    )(qkv, qkv, qkv, ext)


POOL_TM = 512
POOL_HALO = 8


def _pool_kernel(tiles_per_seq, x_ref, prev_ref, next_ref, g_ref, w_ref, scale_ref, gn_ref,
                 o_ref, xg_ref, ssq_ref, hext_ref, wb_ref):
    i = pl.program_id(0)

    @pl.when(i == 0)
    def _():
        wb_ref[...] = w_ref[...].astype(wb_ref.dtype)

    def rms(v):
        return v * lax.rsqrt(jnp.mean(v * v, axis=-1, keepdims=True) + RMS_EPS) * g_ref[...]

    ti = i % tiles_per_seq
    seq = tiles_per_seq * POOL_TM
    x = x_ref[...]
    h = rms(x)
    has_prev = (ti > 0).astype(_F32)
    has_next = (ti < tiles_per_seq - 1).astype(_F32)
    hext_ref[pl.ds(0, POOL_HALO), :] = rms(prev_ref[...]) * has_prev
    hext_ref[pl.ds(POOL_HALO, POOL_TM), :] = h
    hext_ref[pl.ds(POOL_HALO + POOL_TM, POOL_HALO), :] = rms(next_ref[...]) * has_next
    t = ti * POOL_TM + lax.broadcasted_iota(jnp.int32, (POOL_TM, 1), 0)
    outs = []
    for gi, w in enumerate(POOL_WINDOWS):
        cols = slice(gi * GROUP_CH, (gi + 1) * GROUP_CH)
        lo = jnp.clip(t - w // 2, 0, seq)
        hi = jnp.clip(t + w - w // 2, 0, seq)
        cnt = (hi - lo).astype(_F32)
        acc = hext_ref[pl.ds(POOL_HALO - w // 2, POOL_TM), cols]
        for d in range(1 - w // 2, w - w // 2):
            acc = acc + hext_ref[pl.ds(POOL_HALO + d, POOL_TM), cols]
        mixed = (acc / cnt - h[:, cols]).astype(_BF16)
        outs.append(x[:, cols] + _dot(mixed, wb_ref[gi]) * scale_ref[:, cols])
    x_new = jnp.concatenate(outs, axis=1)
    o_ref[...] = x_new
    _emit_norm_inputs(x_new, gn_ref, xg_ref, ssq_ref)


def multiscale_pool(x, g, w_pool, scale, g_next, seq):
    m, d = x.shape
    tiles_per_seq = seq // POOL_TM
    halo_blocks = POOL_TM // POOL_HALO
    last_halo = m // POOL_HALO - 1
    vec = pl.BlockSpec((1, d), lambda i: (0, 0))
    return pl.pallas_call(
        functools.partial(_pool_kernel, tiles_per_seq),
        grid=(m // POOL_TM,),
        in_specs=[pl.BlockSpec((POOL_TM, d), lambda i: (i, 0)),
                  pl.BlockSpec((POOL_HALO, d), lambda i: (jnp.maximum(i * halo_blocks - 1, 0), 0)),
                  pl.BlockSpec((POOL_HALO, d),
                               lambda i: (jnp.minimum((i + 1) * halo_blocks, last_halo), 0)),
                  vec,
                  pl.BlockSpec(w_pool.shape, lambda i: (0, 0, 0)),
                  vec, vec],
        out_specs=[pl.BlockSpec((POOL_TM, d), lambda i: (i, 0)),
                   pl.BlockSpec((POOL_TM, d), lambda i: (i, 0)),
                   pl.BlockSpec((POOL_TM, LANES), lambda i: (i, 0))],
        out_shape=[jax.ShapeDtypeStruct((m, d), _F32), jax.ShapeDtypeStruct((m, d), _BF16),
                   jax.ShapeDtypeStruct((m, LANES), _F32)],
        scratch_shapes=[pltpu.VMEM((POOL_TM + 2 * POOL_HALO, d), _F32),
                        pltpu.VMEM(w_pool.shape, _BF16)],
        compiler_params=_params("arbitrary"),
        name="multiscale_pool",
    )(x, x, x, g.reshape(1, d), w_pool, scale.reshape(1, d), g_next.reshape(1, d))


def kernel(x, p, attn_norm_g, w_qkv, b_qkv, w_o, rpb, pool_norm_g, w_pool, pool_scale,
           ffn_norm_g, w_gate, w_up, w_down, ple_norm_g, w_ple_gate, b_ple_gate,
           w_ple_proj, final_norm_g):
    batch, seq, d = x.shape
    depth = p.shape[0]
    rows = seq // GRID_W
    assert d == D_MODEL and seq % GRID_W == 0 and rows >= WIN_H and seq % POOL_TM == 0
    m = batch * seq
    x = x.reshape(m, d)
    p = p.reshape(depth, m, p.shape[-1])
    q_scale = jnp.concatenate([jnp.full((d,), HEAD_DIM ** -0.5, _F32), jnp.ones((2 * d,), _F32)])

    xg, ssq = norm_inputs(x, attn_norm_g[0])
    for i in range(depth):
        j = i // 2
        if i % 2 == 0:
            qkv = mm_qkv(xg, ssq, w_qkv[j], b_qkv[j], q_scale)
            o = neighbourhood_attention(qkv, rpb[j], batch, seq)
            x, xg, ssq = mm_residual(o, w_o[j], x, ffn_norm_g[i], tn=1024)
        else:
            x, xg, ssq = multiscale_pool(x, pool_norm_g[j], w_pool[j], pool_scale[j],
                                         ffn_norm_g[i], seq)
        hidden = mm_swiglu(xg, ssq, w_gate[i], w_up[i])
        x, xg, ssq = mm_residual(hidden, w_down[i], x, ple_norm_g[i])
        if i + 1 < depth and (i + 1) % 2 == 0:
            x, xg, ssq = mm_ple(xg, ssq, w_ple_gate[i], b_ple_gate[i], p, i, w_ple_proj[i], x,
                                attn_norm_g[(i + 1) // 2])
        else:
            x = mm_ple(xg, ssq, w_ple_gate[i], b_ple_gate[i], p, i, w_ple_proj[i], x, None)
    return rmsnorm(x, final_norm_g).reshape(batch, seq, d)
```

```python
import functools

import jax
import jax.numpy as jnp
import numpy as np
from jax import lax
from jax.experimental import pallas as pl
from jax.experimental.pallas import tpu as pltpu

D_MODEL = 2048
GRID_W = 64
N_HEADS = 64
HEAD_DIM = D_MODEL // N_HEADS
WIN_H = 8
WIN_W = 16
POOL_WINDOWS = (2, 4, 8, 16)
GROUP_CH = D_MODEL // len(POOL_WINDOWS)
RMS_EPS = 1e-6

VMEM_LIMIT_BYTES = 56 * 1024 * 1024
LANES = 128

HEADS_PER_GROUP = 8
GROUP_LANES = HEADS_PER_GROUP * HEAD_DIM
KEY_COLS = 32
Q_GROUPS = ((0, 24, 0), (24, 16, 16), (40, 24, 32))
for _q0, _nq, _k0 in Q_GROUPS:
    _cs = np.clip(np.arange(_q0, _q0 + _nq) - WIN_W // 2, 0, GRID_W - WIN_W)
    assert _cs.min() >= _k0 and _cs.max() + WIN_W <= _k0 + KEY_COLS
MASK_BIAS = -1e30

_BF16 = jnp.bfloat16
_F32 = jnp.float32


def _params(*semantics):
    return pltpu.CompilerParams(dimension_semantics=semantics,
                                vmem_limit_bytes=VMEM_LIMIT_BYTES)


def _dot(a, b):
    return jnp.dot(a, b, preferred_element_type=_F32)


def _row_scale(ssq_ref, d):
    total = ssq_ref[:, 0:LANES]
    for part in range(1, ssq_ref.shape[1] // LANES):
        total = total + ssq_ref[:, part * LANES:(part + 1) * LANES]
    return lax.rsqrt(total[:, 0:1] * (1.0 / d) + RMS_EPS)


def _emit_norm_inputs(x, g_ref, xg_ref, ssq_ref):
    xg_ref[...] = (x * g_ref[...]).astype(xg_ref.dtype)
    ssq_ref[...] = jnp.broadcast_to(jnp.sum(x * x, axis=-1, keepdims=True), ssq_ref.shape)


def _prep_kernel(x_ref, g_ref, xg_ref, ssq_ref):
    _emit_norm_inputs(x_ref[...], g_ref, xg_ref, ssq_ref)


def norm_inputs(x, g, tm=512):
    m, d = x.shape
    return pl.pallas_call(
        _prep_kernel,
        grid=(m // tm,),
        in_specs=[pl.BlockSpec((tm, d), lambda i: (i, 0)),
                  pl.BlockSpec((1, d), lambda i: (0, 0))],
        out_specs=[pl.BlockSpec((tm, d), lambda i: (i, 0)),
                   pl.BlockSpec((tm, LANES), lambda i: (i, 0))],
        out_shape=[jax.ShapeDtypeStruct((m, d), _BF16),
                   jax.ShapeDtypeStruct((m, LANES), _F32)],
        compiler_params=_params("parallel"),
        name="norm_inputs",
    )(x, g.reshape(1, d))


def _rmsnorm_kernel(x_ref, g_ref, o_ref):
    x = x_ref[...]
    o_ref[...] = x * lax.rsqrt(jnp.mean(x * x, axis=-1, keepdims=True) + RMS_EPS) * g_ref[...]


def rmsnorm(x, g, tm=512):
    m, d = x.shape
    return pl.pallas_call(
        _rmsnorm_kernel,
        grid=(m // tm,),
        in_specs=[pl.BlockSpec((tm, d), lambda i: (i, 0)),
                  pl.BlockSpec((1, d), lambda i: (0, 0))],
        out_specs=pl.BlockSpec((tm, d), lambda i: (i, 0)),
        out_shape=jax.ShapeDtypeStruct((m, d), _F32),
        compiler_params=_params("parallel"),
        name="rmsnorm",
    )(x, g.reshape(1, d))


CAST_ROWS = 256


def _cast_weight(w_ref, wb_ref):
    @pl.when(pl.program_id(1) == 0)
    def _():
        def body(c, carry):
            rows = pl.ds(pl.multiple_of(c * CAST_ROWS, CAST_ROWS), CAST_ROWS)
            wb_ref[rows, :] = w_ref[rows, :].astype(wb_ref.dtype)
            return carry
        lax.fori_loop(0, w_ref.shape[0] // CAST_ROWS, body, 0)


def _col_halves(ref):
    half = ref.shape[1] // 2
    return (slice(0, half), slice(half, 2 * half))


def _emit_norm_cols(x, cols, g_ref, xg_ref):
    xg_ref[:, cols] = (x * g_ref[:, cols]).astype(xg_ref.dtype)
    return jnp.sum(x * x, axis=-1, keepdims=True)


def _mm_qkv_kernel(d, xg_ref, ssq_ref, w_ref, b_ref, s_ref, o_ref, wb_ref):
    _cast_weight(w_ref, wb_ref)
    a = xg_ref[...]
    r = _row_scale(ssq_ref, d)
    for cols in _col_halves(o_ref):
        acc = _dot(a, wb_ref[:, cols]) * r
        o_ref[:, cols] = ((acc + b_ref[:, cols]) * s_ref[:, cols]).astype(o_ref.dtype)


def _mm_swiglu_kernel(d, xg_ref, ssq_ref, wg_ref, wu_ref, o_ref, wgb_ref, wub_ref):
    _cast_weight(wg_ref, wgb_ref)
    _cast_weight(wu_ref, wub_ref)
    a = xg_ref[...]
    r = _row_scale(ssq_ref, d)
    for cols in _col_halves(o_ref):
        gate = _dot(a, wgb_ref[:, cols]) * r
        up = _dot(a, wub_ref[:, cols]) * r
        o_ref[:, cols] = (gate * jax.nn.sigmoid(gate) * up).astype(o_ref.dtype)


def _mm_residual_kernel(a_ref, w_ref, res_ref, g_ref, x_ref, xg_ref, ssq_ref, wb_ref):
    _cast_weight(w_ref, wb_ref)
    a = a_ref[...]
    ssq = 0.0
    for cols in _col_halves(x_ref):
        x = res_ref[:, cols] + _dot(a, wb_ref[:, cols])
        x_ref[:, cols] = x
        ssq = ssq + _emit_norm_cols(x, cols, g_ref, xg_ref)
    ssq_ref[...] = jnp.broadcast_to(ssq, ssq_ref.shape)


def _mm_ple_kernel(d, emit, xg_ref, ssq_ref, wg_ref, b_ref, p_ref, wp_ref, res_ref, *rest):
    if emit:
        g_ref, x_ref, xg_out_ref, ssq_out_ref, wgb_ref, wpb_ref = rest
    else:
        x_ref, wgb_ref, wpb_ref = rest
    _cast_weight(wg_ref, wgb_ref)
    _cast_weight(wp_ref, wpb_ref)
    a = xg_ref[...]
    pb = p_ref[...].astype(_BF16)
    r = _row_scale(ssq_ref, d)
    ssq = 0.0
    for cols in _col_halves(x_ref):
        gate = jax.nn.sigmoid(_dot(a, wgb_ref[:, cols]) * r + b_ref[:, cols])
        x = res_ref[:, cols] + _dot(pb, wpb_ref[:, cols]) * gate
        x_ref[:, cols] = x
        if emit:
            ssq = ssq + _emit_norm_cols(x, cols, g_ref, xg_out_ref)
    if emit:
        ssq_out_ref[...] = jnp.broadcast_to(ssq, ssq_out_ref.shape)


def _row_spec(tm, k):
    return pl.BlockSpec((tm, k), lambda j, i: (i, 0))


def _col_spec(layer, k, tn):
    return pl.BlockSpec((None, k, tn), lambda j, i: (layer, 0, j))


def _vec_spec(tn):
    return pl.BlockSpec((1, tn), lambda j, i: (0, j))


def _tile_spec(tm, tn):
    return pl.BlockSpec((tm, tn), lambda j, i: (i, j))


def _ssq_out_spec(tm):
    return pl.BlockSpec((tm, LANES), lambda j, i: (i, j))


def _mm_params():
    return _params("parallel", "arbitrary")


def mm_qkv(xg, ssq, w, layer, b, s, tm=1024, tn=1024):
    m, k = xg.shape
    n = w.shape[-1]
    return pl.pallas_call(
        functools.partial(_mm_qkv_kernel, k),
        grid=(n // tn, m // tm),
        in_specs=[_row_spec(tm, k), _row_spec(tm, ssq.shape[1]), _col_spec(layer, k, tn),
                  _vec_spec(tn), _vec_spec(tn)],
        out_specs=_tile_spec(tm, tn),
        out_shape=jax.ShapeDtypeStruct((m, n), _BF16),
        scratch_shapes=[pltpu.VMEM((k, tn), _BF16)],
        compiler_params=_mm_params(),
        name="mm_qkv",
    )(xg, ssq, w, b.reshape(1, n), s.reshape(1, n))


def mm_swiglu(xg, ssq, wg, wu, layer, tm=1024, tn=512):
    m, k = xg.shape
    n = wg.shape[-1]
    return pl.pallas_call(
        functools.partial(_mm_swiglu_kernel, k),
        grid=(n // tn, m // tm),
        in_specs=[_row_spec(tm, k), _row_spec(tm, ssq.shape[1]),
                  _col_spec(layer, k, tn), _col_spec(layer, k, tn)],
        out_specs=_tile_spec(tm, tn),
        out_shape=jax.ShapeDtypeStruct((m, n), _BF16),
        scratch_shapes=[pltpu.VMEM((k, tn), _BF16), pltpu.VMEM((k, tn), _BF16)],
        compiler_params=_mm_params(),
        name="mm_swiglu",
    )(xg, ssq, wg, wu)


def _norm_out(m, n, tm, tn):
    specs = [_tile_spec(tm, tn), _tile_spec(tm, tn), _ssq_out_spec(tm)]
    shapes = [jax.ShapeDtypeStruct((m, n), _F32), jax.ShapeDtypeStruct((m, n), _BF16),
              jax.ShapeDtypeStruct((m, LANES * (n // tn)), _F32)]
    return specs, shapes


def mm_residual(a, w, layer, res, g_next, tm=512, tn=512):
    m, k = a.shape
    n = w.shape[-1]
    out_specs, out_shape = _norm_out(m, n, tm, tn)
    return pl.pallas_call(
        _mm_residual_kernel,
        grid=(n // tn, m // tm),
        in_specs=[_row_spec(tm, k), _col_spec(layer, k, tn), _tile_spec(tm, tn), _vec_spec(tn)],
        out_specs=out_specs,
        out_shape=out_shape,
        scratch_shapes=[pltpu.VMEM((k, tn), _BF16)],
        compiler_params=_mm_params(),
        name="mm_residual",
    )(a, w, res, g_next.reshape(1, n))


def mm_ple(xg, ssq, wg, b, p, wp, layer, res, g_next, tm=512, tn=1024):
    m, k = xg.shape
    n = wg.shape[-1]
    kp = p.shape[-1]
    emit = g_next is not None
    in_specs = [_row_spec(tm, k), _row_spec(tm, ssq.shape[1]), _col_spec(layer, k, tn), _vec_spec(tn),
                pl.BlockSpec((None, tm, kp), lambda j, i: (layer, i, 0)), _col_spec(layer, kp, tn),
                _tile_spec(tm, tn)]
    args = [xg, ssq, wg, b.reshape(1, n), p, wp, res]
    if emit:
        in_specs.append(_vec_spec(tn))
        args.append(g_next.reshape(1, n))
        out_specs, out_shape = _norm_out(m, n, tm, tn)
    else:
        out_specs, out_shape = _tile_spec(tm, tn), jax.ShapeDtypeStruct((m, n), _F32)
    return pl.pallas_call(
        functools.partial(_mm_ple_kernel, k, emit),
        grid=(n // tn, m // tm),
        in_specs=in_specs,
        out_specs=out_specs,
        out_shape=out_shape,
        scratch_shapes=[pltpu.VMEM((k, tn), _BF16), pltpu.VMEM((kp, tn), _BF16)],
        compiler_params=_mm_params(),
        name="mm_ple",
    )(*args)


EXT_PAD = KEY_COLS // 2
N_BIAS_ROWS = 2 * WIN_H - 1
N_QUADS = N_BIAS_ROWS - LANES // KEY_COLS + 1
GROUP_ROWS = HEADS_PER_GROUP * GRID_W


def _build_bias_quads(ext_ref, quad_ref):
    lane = lax.broadcasted_iota(jnp.int32, (GRID_W, LANES), 1)
    qc = lax.broadcasted_iota(jnp.int32, (GRID_W, LANES), 0)
    lane_j = lane // KEY_COLS
    k0 = jnp.where(qc < Q_GROUPS[1][0], Q_GROUPS[0][2],
                   jnp.where(qc < Q_GROUPS[2][0], Q_GROUPS[1][2], Q_GROUPS[2][2]))
    kc = k0 + lane % KEY_COLS
    c_start = jnp.clip(qc - WIN_W // 2, 0, GRID_W - WIN_W)
    valid = (kc >= c_start) & (kc < c_start + WIN_W)

    def head_body(h, carry):
        def rolled(dr, j):
            row = ext_ref[0, pl.ds(h * N_BIAS_ROWS + dr, 1), :]
            parts = []
            for q0, nq, k0_g in Q_GROUPS:
                shift = (KEY_COLS * j - k0_g - (WIN_W - 1) - EXT_PAD + q0) % LANES
                parts.append(pltpu.roll(jnp.broadcast_to(row, (nq, LANES)), shift, 1,
                                        stride=1, stride_axis=0))
            return jnp.concatenate(parts, axis=0)

        for d0 in range(N_QUADS):
            quad = rolled(d0, 0)
            for j in range(1, LANES // KEY_COLS):
                quad = jnp.where(lane_j == j, rolled(d0 + j, j), quad)
            quad_ref[d0, pl.ds(pl.multiple_of(h * GRID_W, GRID_W), GRID_W), :] = (
                jnp.where(valid, quad, MASK_BIAS))
        return carry

    lax.fori_loop(0, HEADS_PER_GROUP, head_body, 0)


def _attn_kernel(q_ref, k_ref, v_ref, ext_ref, o_ref, quad_ref,
                 s0_ref, s1_ref, e0_ref, e1_ref, l0_ref, l1_ref):
    rows = q_ref.shape[0] // GRID_W
    lane_head = lax.broadcasted_iota(jnp.int32, (1, GROUP_LANES), 1) // HEAD_DIM
    head_masks = [lane_head == h for h in range(HEADS_PER_GROUP)]

    @pl.when(pl.program_id(1) == 0)
    def _():
        _build_bias_quads(ext_ref, quad_ref)

    def window(ref, r, k0):
        r_start = jnp.clip(r - WIN_H // 2, 0, rows - WIN_H)
        starts = [pl.multiple_of((r_start + j) * GRID_W + k0, 16) for j in range(WIN_H)]
        return jnp.concatenate([ref[pl.ds(s, KEY_COLS), :] for s in starts], axis=0)

    def score_stage(r, s_ref):
        d0 = jnp.clip(r - WIN_H // 2, 0, rows - WIN_H) - r + (WIN_H - 1)
        q_row = q_ref[pl.ds(pl.multiple_of(r * GRID_W, GRID_W), GRID_W), :].astype(_F32)
        off = 0
        for q0, nq, k0 in Q_GROUPS:
            qg = q_row[q0:q0 + nq]
            qm = jnp.concatenate([jnp.where(mask, qg, 0.0) for mask in head_masks],
                                 axis=0).astype(_BF16)
            s = lax.dot_general(qm, window(k_ref, r, k0), (((1,), (1,)), ((), ())),
                                preferred_element_type=_F32)
            bias = jnp.concatenate(
                [jnp.concatenate([quad_ref[d0 + half * (LANES // KEY_COLS),
                                           pl.ds(h * GRID_W + q0, nq), :]
                                  for h in range(HEADS_PER_GROUP)], axis=0)
                 for half in range(2)], axis=1)
            s_ref[pl.ds(off, HEADS_PER_GROUP * nq), :] = s + bias
            off += HEADS_PER_GROUP * nq

    def exp_stage(s_ref, e_ref, l_ref):
        off = 0
        for _, nq, _ in Q_GROUPS:
            rows_g = pl.ds(off, HEADS_PER_GROUP * nq)
            s = s_ref[rows_g, :]
            e = jnp.exp(s - jnp.max(s, axis=-1, keepdims=True))
            e_ref[rows_g, :] = e.astype(e_ref.dtype)
            l_ref[rows_g, :] = jnp.broadcast_to(1.0 / jnp.sum(e, axis=-1, keepdims=True),
                                                (HEADS_PER_GROUP * nq, LANES))
            off += HEADS_PER_GROUP * nq

    def value_stage(r, e_ref, l_ref):
        outs = []
        off = 0
        for q0, nq, k0 in Q_GROUPS:
            rows_g = pl.ds(off, HEADS_PER_GROUP * nq)
            o_all = _dot(e_ref[rows_g, :], window(v_ref, r, k0))
            inv = l_ref[rows_g, :]
            o_all = o_all * jnp.concatenate([inv, inv], axis=1)
            o = jnp.where(head_masks[0], o_all[:nq], 0.0)
            for h in range(1, HEADS_PER_GROUP):
                o = o + jnp.where(head_masks[h], o_all[h * nq:(h + 1) * nq], 0.0)
            outs.append(o)
            off += HEADS_PER_GROUP * nq
        o_ref[pl.ds(pl.multiple_of(r * GRID_W, GRID_W), GRID_W), :] = (
            jnp.concatenate(outs, axis=0).astype(o_ref.dtype))

    score_stage(0, s0_ref)
    score_stage(1, s1_ref)
    exp_stage(s0_ref, e0_ref, l0_ref)

    def pair_body(k, carry):
        r = 2 * k
        score_stage(r, s0_ref)
        exp_stage(s1_ref, e1_ref, l1_ref)
        value_stage(r - 2, e0_ref, l0_ref)
        score_stage(r + 1, s1_ref)
        exp_stage(s0_ref, e0_ref, l0_ref)
        value_stage(r - 1, e1_ref, l1_ref)
        return carry

    lax.fori_loop(1, rows // 2, pair_body, 0)
    exp_stage(s1_ref, e1_ref, l1_ref)
    value_stage(rows - 2, e0_ref, l0_ref)
    value_stage(rows - 1, e1_ref, l1_ref)


def _attention_bias_ext(rpb):
    ext = jnp.pad(rpb, ((0, 0), (0, 0), (EXT_PAD, EXT_PAD)), mode="edge")
    ext = jnp.pad(ext, ((0, 0), (0, 0), (0, LANES - ext.shape[-1])))
    return ext.reshape(N_HEADS // HEADS_PER_GROUP, HEADS_PER_GROUP * N_BIAS_ROWS, LANES)


def neighbourhood_attention(qkv, rpb, batch, seq):
    n_hg = N_HEADS // HEADS_PER_GROUP
    rows = seq // GRID_W
    assert rows % 2 == 0 and rows >= max(WIN_H, 4)
    blk = (seq, GROUP_LANES)
    ext = _attention_bias_ext(rpb)
    score_scratch = pltpu.VMEM((GROUP_ROWS, WIN_H * KEY_COLS), _F32)
    exp_scratch = pltpu.VMEM((GROUP_ROWS, WIN_H * KEY_COLS), _BF16)
    inv_scratch = pltpu.VMEM((GROUP_ROWS, LANES), _F32)
    return pl.pallas_call(
        _attn_kernel,
        grid=(n_hg, batch),
        in_specs=[pl.BlockSpec(blk, lambda hg, b: (b, hg)),
                  pl.BlockSpec(blk, lambda hg, b: (b, n_hg + hg)),
                  pl.BlockSpec(blk, lambda hg, b: (b, 2 * n_hg + hg)),
                  pl.BlockSpec((1,) + ext.shape[1:], lambda hg, b: (hg, 0, 0))],
        out_specs=pl.BlockSpec(blk, lambda hg, b: (b, hg)),
        out_shape=jax.ShapeDtypeStruct((batch * seq, D_MODEL), _BF16),
        scratch_shapes=[pltpu.VMEM((N_QUADS, GROUP_ROWS, LANES), _F32),
                        score_scratch, score_scratch, exp_scratch, exp_scratch,
                        inv_scratch, inv_scratch],
        compiler_params=_params("arbitrary", "arbitrary"),
        name="nbr_attention",
    )(qkv, qkv, qkv, ext)


POOL_TM = 512
POOL_HALO = 8


def _pool_kernel(tiles_per_seq, x_ref, prev_ref, next_ref, g_ref, w_ref, scale_ref, gn_ref,
                 o_ref, xg_ref, ssq_ref, hext_ref, mixed_ref, wb_ref):
    i = pl.program_id(0)
    n_ext = POOL_TM + 2 * POOL_HALO
    inner = slice(POOL_HALO, POOL_HALO + POOL_TM)

    def ahead(v, s):
        return pltpu.roll(v, n_ext - s, 0)

    @pl.when(i == 0)
    def _():
        wb_ref[...] = w_ref[...].astype(wb_ref.dtype)

    def rms(v):
        return v * lax.rsqrt(jnp.mean(v * v, axis=-1, keepdims=True) + RMS_EPS) * g_ref[...]

    ti = i % tiles_per_seq
    seq = tiles_per_seq * POOL_TM
    x = x_ref[...]
    h = rms(x)
    has_prev = (ti > 0).astype(_F32)
    has_next = (ti < tiles_per_seq - 1).astype(_F32)
    hext_ref[pl.ds(0, POOL_HALO), :] = rms(prev_ref[...]) * has_prev
    hext_ref[pl.ds(POOL_HALO, POOL_TM), :] = h
    hext_ref[pl.ds(POOL_HALO + POOL_TM, POOL_HALO), :] = rms(next_ref[...]) * has_next
    t = ti * POOL_TM + lax.broadcasted_iota(jnp.int32, (POOL_TM, 1), 0)
    outs = []
    for gi, w in enumerate(POOL_WINDOWS):
        half = w // 2
        assert half <= POOL_HALO
        lo = jnp.clip(t - half, 0, seq)
        hi = jnp.clip(t + w - half, 0, seq)
        inv_cnt = 1.0 / (hi - lo).astype(_F32)
        for c in range(GROUP_CH // LANES):
            cols = slice(gi * GROUP_CH + c * LANES, gi * GROUP_CH + (c + 1) * LANES)
            hx = hext_ref[:, cols]
            run, span = hx, 1
            while span < half:
                run = run + ahead(run, span)
                span *= 2
            first = run if half == POOL_HALO else ahead(run, POOL_HALO - half)
            win = first[:POOL_TM] + run[inner]
            mixed_ref[:, cols] = (win * inv_cnt - hx[inner]).astype(mixed_ref.dtype)
        cols = slice(gi * GROUP_CH, (gi + 1) * GROUP_CH)
        outs.append(x[:, cols] + _dot(mixed_ref[:, cols], wb_ref[gi]) * scale_ref[:, cols])
    x_new = jnp.concatenate(outs, axis=1)
    o_ref[...] = x_new
    _emit_norm_inputs(x_new, gn_ref, xg_ref, ssq_ref)


def multiscale_pool(x, g, w_pool, layer, scale, g_next, seq):
    m, d = x.shape
    tiles_per_seq = seq // POOL_TM
    halo_blocks = POOL_TM // POOL_HALO
    last_halo = m // POOL_HALO - 1
    vec = pl.BlockSpec((1, d), lambda i: (0, 0))
    return pl.pallas_call(
        functools.partial(_pool_kernel, tiles_per_seq),
        grid=(m // POOL_TM,),
        in_specs=[pl.BlockSpec((POOL_TM, d), lambda i: (i, 0)),
                  pl.BlockSpec((POOL_HALO, d), lambda i: (jnp.maximum(i * halo_blocks - 1, 0), 0)),
                  pl.BlockSpec((POOL_HALO, d),
                               lambda i: (jnp.minimum((i + 1) * halo_blocks, last_halo), 0)),
                  vec,
                  pl.BlockSpec((None,) + w_pool.shape[1:], lambda i: (layer, 0, 0, 0)),
                  vec, vec],
        out_specs=[pl.BlockSpec((POOL_TM, d), lambda i: (i, 0)),
                   pl.BlockSpec((POOL_TM, d), lambda i: (i, 0)),
                   pl.BlockSpec((POOL_TM, LANES), lambda i: (i, 0))],
        out_shape=[jax.ShapeDtypeStruct((m, d), _F32), jax.ShapeDtypeStruct((m, d), _BF16),
                   jax.ShapeDtypeStruct((m, LANES), _F32)],
        scratch_shapes=[pltpu.VMEM((POOL_TM + 2 * POOL_HALO, d), _F32),
                        pltpu.VMEM((POOL_TM, d), _BF16),
                        pltpu.VMEM(w_pool.shape[1:], _BF16)],
        compiler_params=_params("arbitrary"),
        name="multiscale_pool",
    )(x, x, x, g.reshape(1, d), w_pool, scale.reshape(1, d), g_next.reshape(1, d))


def kernel(x, p, attn_norm_g, w_qkv, b_qkv, w_o, rpb, pool_norm_g, w_pool, pool_scale,
           ffn_norm_g, w_gate, w_up, w_down, ple_norm_g, w_ple_gate, b_ple_gate,
           w_ple_proj, final_norm_g):
    batch, seq, d = x.shape
    depth = p.shape[0]
    rows = seq // GRID_W
    assert d == D_MODEL and seq % GRID_W == 0 and rows >= WIN_H and seq % POOL_TM == 0
    m = batch * seq
    x = x.reshape(m, d)
    p = p.reshape(depth, m, p.shape[-1])
    q_scale = jnp.concatenate([jnp.full((d,), HEAD_DIM ** -0.5, _F32), jnp.ones((2 * d,), _F32)])

    xg, ssq = norm_inputs(x, attn_norm_g[0])
    for i in range(depth):
        j = i // 2
        if i % 2 == 0:
            qkv = mm_qkv(xg, ssq, w_qkv, j, b_qkv[j], q_scale)
            o = neighbourhood_attention(qkv, rpb[j], batch, seq)
            x, xg, ssq = mm_residual(o, w_o, j, x, ffn_norm_g[i], tn=1024)
        else:
            x, xg, ssq = multiscale_pool(x, pool_norm_g[j], w_pool, j, pool_scale[j],
                                         ffn_norm_g[i], seq)
        hidden = mm_swiglu(xg, ssq, w_gate, w_up, i)
        x, xg, ssq = mm_residual(hidden, w_down, i, x, ple_norm_g[i])
        next_is_attn = i + 1 < depth and (i + 1) % 2 == 0
        g_next = attn_norm_g[(i + 1) // 2] if next_is_attn else None
        out = mm_ple(xg, ssq, w_ple_gate, b_ple_gate[i], p, w_ple_proj, i, x, g_next)
        x, xg, ssq = out if next_is_attn else (out, None, None)
    return rmsnorm(x, final_norm_g).reshape(batch, seq, d)
```

```python
import functools

import jax
import jax.numpy as jnp
import numpy as np
from jax import lax
from jax.experimental import pallas as pl
from jax.experimental.pallas import tpu as pltpu

D_MODEL = 2048
GRID_W = 64
N_HEADS = 64
HEAD_DIM = D_MODEL // N_HEADS
WIN_H = 8
WIN_W = 16
POOL_WINDOWS = (2, 4, 8, 16)
GROUP_CH = D_MODEL // len(POOL_WINDOWS)
RMS_EPS = 1e-6

VMEM_LIMIT_BYTES = 56 * 1024 * 1024
LANES = 128

HEADS_PER_GROUP = 8
GROUP_LANES = HEADS_PER_GROUP * HEAD_DIM
KEY_COLS = 32
Q_GROUPS = ((0, 24, 0), (24, 16, 16), (40, 24, 32))
for _q0, _nq, _k0 in Q_GROUPS:
    _cs = np.clip(np.arange(_q0, _q0 + _nq) - WIN_W // 2, 0, GRID_W - WIN_W)
    assert _cs.min() >= _k0 and _cs.max() + WIN_W <= _k0 + KEY_COLS
MASK_BIAS = -1e30
LOG2_E = float(np.log2(np.e))

_BF16 = jnp.bfloat16
_F32 = jnp.float32


def _params(*semantics):
    return pltpu.CompilerParams(dimension_semantics=semantics,
                                vmem_limit_bytes=VMEM_LIMIT_BYTES)


def _dot(a, b):
    return jnp.dot(a, b, preferred_element_type=_F32)


def _row_scale(ssq_ref, d):
    total = ssq_ref[:, 0:LANES]
    for part in range(1, ssq_ref.shape[1] // LANES):
        total = total + ssq_ref[:, part * LANES:(part + 1) * LANES]
    return lax.rsqrt(total[:, 0:1] * (1.0 / d) + RMS_EPS)


def _emit_norm_inputs(x, g_ref, xg_ref, ssq_ref):
    xg_ref[...] = (x * g_ref[...]).astype(xg_ref.dtype)
    ssq_ref[...] = jnp.broadcast_to(jnp.sum(x * x, axis=-1, keepdims=True), ssq_ref.shape)


def _prep_kernel(x_ref, g_ref, xg_ref, ssq_ref):
    _emit_norm_inputs(x_ref[...], g_ref, xg_ref, ssq_ref)


def norm_inputs(x, g, tm=512):
    m, d = x.shape
    return pl.pallas_call(
        _prep_kernel,
        grid=(m // tm,),
        in_specs=[pl.BlockSpec((tm, d), lambda i: (i, 0)),
                  pl.BlockSpec((1, d), lambda i: (0, 0))],
        out_specs=[pl.BlockSpec((tm, d), lambda i: (i, 0)),
                   pl.BlockSpec((tm, LANES), lambda i: (i, 0))],
        out_shape=[jax.ShapeDtypeStruct((m, d), _BF16),
                   jax.ShapeDtypeStruct((m, LANES), _F32)],
        compiler_params=_params("parallel"),
        name="norm_inputs",
    )(x, g.reshape(1, d))


def _rmsnorm_kernel(x_ref, g_ref, o_ref):
    x = x_ref[...]
    o_ref[...] = x * lax.rsqrt(jnp.mean(x * x, axis=-1, keepdims=True) + RMS_EPS) * g_ref[...]


def rmsnorm(x, g, tm=512):
    m, d = x.shape
    return pl.pallas_call(
        _rmsnorm_kernel,
        grid=(m // tm,),
        in_specs=[pl.BlockSpec((tm, d), lambda i: (i, 0)),
                  pl.BlockSpec((1, d), lambda i: (0, 0))],
        out_specs=pl.BlockSpec((tm, d), lambda i: (i, 0)),
        out_shape=jax.ShapeDtypeStruct((m, d), _F32),
        compiler_params=_params("parallel"),
        name="rmsnorm",
    )(x, g.reshape(1, d))


CAST_ROWS = 256


def _cast_weight(w_ref, wb_ref):
    @pl.when(pl.program_id(1) == 0)
    def _():
        def body(c, carry):
            rows = pl.ds(pl.multiple_of(c * CAST_ROWS, CAST_ROWS), CAST_ROWS)
            wb_ref[rows, :] = w_ref[rows, :].astype(wb_ref.dtype)
            return carry
        lax.fori_loop(0, w_ref.shape[0] // CAST_ROWS, body, 0)


def _col_halves(ref):
    half = ref.shape[1] // 2
    return (slice(0, half), slice(half, 2 * half))


def _emit_norm_cols(x, cols, g_ref, xg_ref):
    xg_ref[:, cols] = (x * g_ref[:, cols]).astype(xg_ref.dtype)
    return jnp.sum(x * x, axis=-1, keepdims=True)


def _mm_qkv_kernel(d, xg_ref, ssq_ref, w_ref, b_ref, s_ref, o_ref, wb_ref):
    _cast_weight(w_ref, wb_ref)
    a = xg_ref[...]
    r = _row_scale(ssq_ref, d)
    for cols in _col_halves(o_ref):
        acc = _dot(a, wb_ref[:, cols]) * r
        o_ref[:, cols] = ((acc + b_ref[:, cols]) * s_ref[:, cols]).astype(o_ref.dtype)


def _mm_swiglu_kernel(d, xg_ref, ssq_ref, wg_ref, wu_ref, o_ref, wgb_ref, wub_ref):
    _cast_weight(wg_ref, wgb_ref)
    _cast_weight(wu_ref, wub_ref)
    a = xg_ref[...]
    r = _row_scale(ssq_ref, d)
    for cols in _col_halves(o_ref):
        gate = _dot(a, wgb_ref[:, cols]) * r
        up = _dot(a, wub_ref[:, cols]) * r
        o_ref[:, cols] = (gate * jax.nn.sigmoid(gate) * up).astype(o_ref.dtype)


def _mm_residual_kernel(a_ref, w_ref, res_ref, g_ref, x_ref, xg_ref, ssq_ref, wb_ref):
    _cast_weight(w_ref, wb_ref)
    a = a_ref[...]
    ssq = 0.0
    for cols in _col_halves(x_ref):
        x = res_ref[:, cols] + _dot(a, wb_ref[:, cols])
        x_ref[:, cols] = x
        ssq = ssq + _emit_norm_cols(x, cols, g_ref, xg_ref)
    ssq_ref[...] = jnp.broadcast_to(ssq, ssq_ref.shape)


def _mm_ple_kernel(d, emit, xg_ref, ssq_ref, wg_ref, b_ref, p_ref, wp_ref, res_ref, *rest):
    if emit:
        g_ref, x_ref, xg_out_ref, ssq_out_ref, wgb_ref, wpb_ref = rest
    else:
        x_ref, wgb_ref, wpb_ref = rest
    _cast_weight(wg_ref, wgb_ref)
    _cast_weight(wp_ref, wpb_ref)
    a = xg_ref[...]
    pb = p_ref[...].astype(_BF16)
    r = _row_scale(ssq_ref, d)
    ssq = 0.0
    for cols in _col_halves(x_ref):
        gate = jax.nn.sigmoid(_dot(a, wgb_ref[:, cols]) * r + b_ref[:, cols])
        x = res_ref[:, cols] + _dot(pb, wpb_ref[:, cols]) * gate
        x_ref[:, cols] = x
        if emit:
            ssq = ssq + _emit_norm_cols(x, cols, g_ref, xg_out_ref)
    if emit:
        ssq_out_ref[...] = jnp.broadcast_to(ssq, ssq_out_ref.shape)


def _row_spec(tm, k):
    return pl.BlockSpec((tm, k), lambda j, i: (i, 0))


def _col_spec(layer, k, tn, single_buffer=False):
    mode = pl.Buffered(1) if single_buffer else None
    return pl.BlockSpec((None, k, tn), lambda j, i: (layer, 0, j), pipeline_mode=mode)


def _vec_spec(tn):
    return pl.BlockSpec((1, tn), lambda j, i: (0, j))


def _tile_spec(tm, tn):
    return pl.BlockSpec((tm, tn), lambda j, i: (i, j))


def _ssq_out_spec(tm):
    return pl.BlockSpec((tm, LANES), lambda j, i: (i, j))


def _mm_params():
    return _params("parallel", "arbitrary")


def mm_qkv(xg, ssq, w, layer, b, s, tm=1024, tn=1024):
    m, k = xg.shape
    n = w.shape[-1]
    return pl.pallas_call(
        functools.partial(_mm_qkv_kernel, k),
        grid=(n // tn, m // tm),
        in_specs=[_row_spec(tm, k), _row_spec(tm, ssq.shape[1]), _col_spec(layer, k, tn),
                  _vec_spec(tn), _vec_spec(tn)],
        out_specs=_tile_spec(tm, tn),
        out_shape=jax.ShapeDtypeStruct((m, n), _BF16),
        scratch_shapes=[pltpu.VMEM((k, tn), _BF16)],
        compiler_params=_mm_params(),
        name="mm_qkv",
    )(xg, ssq, w, b.reshape(1, n), s.reshape(1, n))


def mm_swiglu(xg, ssq, wg, wu, layer, tm=1024, tn=512):
    m, k = xg.shape
    n = wg.shape[-1]
    return pl.pallas_call(
        functools.partial(_mm_swiglu_kernel, k),
        grid=(n // tn, m // tm),
        in_specs=[_row_spec(tm, k), _row_spec(tm, ssq.shape[1]),
                  _col_spec(layer, k, tn), _col_spec(layer, k, tn)],
        out_specs=_tile_spec(tm, tn),
        out_shape=jax.ShapeDtypeStruct((m, n), _BF16),
        scratch_shapes=[pltpu.VMEM((k, tn), _BF16), pltpu.VMEM((k, tn), _BF16)],
        compiler_params=_mm_params(),
        name="mm_swiglu",
    )(xg, ssq, wg, wu)


def _norm_out(m, n, tm, tn):
    specs = [_tile_spec(tm, tn), _tile_spec(tm, tn), _ssq_out_spec(tm)]
    shapes = [jax.ShapeDtypeStruct((m, n), _F32), jax.ShapeDtypeStruct((m, n), _BF16),
              jax.ShapeDtypeStruct((m, LANES * (n // tn)), _F32)]
    return specs, shapes


def mm_residual(a, w, layer, res, g_next, tm=512, tn=512):
    m, k = a.shape
    n = w.shape[-1]
    out_specs, out_shape = _norm_out(m, n, tm, tn)
    return pl.pallas_call(
        _mm_residual_kernel,
        grid=(n // tn, m // tm),
        in_specs=[_row_spec(tm, k), _col_spec(layer, k, tn), _tile_spec(tm, tn), _vec_spec(tn)],
        out_specs=out_specs,
        out_shape=out_shape,
        scratch_shapes=[pltpu.VMEM((k, tn), _BF16)],
        compiler_params=_mm_params(),
        name="mm_residual",
    )(a, w, res, g_next.reshape(1, n))


def mm_ple(xg, ssq, wg, b, p, wp, layer, res, g_next, tm=1024, tn=1024):
    m, k = xg.shape
    n = wg.shape[-1]
    kp = p.shape[-1]
    emit = g_next is not None
    in_specs = [_row_spec(tm, k), _row_spec(tm, ssq.shape[1]),
                _col_spec(layer, k, tn, single_buffer=True), _vec_spec(tn),
                pl.BlockSpec((None, tm, kp), lambda j, i: (layer, i, 0)),
                _col_spec(layer, kp, tn, single_buffer=True), _tile_spec(tm, tn)]
    args = [xg, ssq, wg, b.reshape(1, n), p, wp, res]
    if emit:
        in_specs.append(_vec_spec(tn))
        args.append(g_next.reshape(1, n))
        out_specs, out_shape = _norm_out(m, n, tm, tn)
    else:
        out_specs, out_shape = _tile_spec(tm, tn), jax.ShapeDtypeStruct((m, n), _F32)
    return pl.pallas_call(
        functools.partial(_mm_ple_kernel, k, emit),
        grid=(n // tn, m // tm),
        in_specs=in_specs,
        out_specs=out_specs,
        out_shape=out_shape,
        scratch_shapes=[pltpu.VMEM((k, tn), _BF16), pltpu.VMEM((kp, tn), _BF16)],
        compiler_params=_mm_params(),
        name="mm_ple",
    )(*args)


EXT_PAD = KEY_COLS // 2
N_BIAS_ROWS = 2 * WIN_H - 1
N_QUADS = N_BIAS_ROWS - LANES // KEY_COLS + 1
GROUP_ROWS = HEADS_PER_GROUP * GRID_W


def _build_bias_quads(ext_ref, quad_ref):
    lane = lax.broadcasted_iota(jnp.int32, (GRID_W, LANES), 1)
    qc = lax.broadcasted_iota(jnp.int32, (GRID_W, LANES), 0)
    lane_j = lane // KEY_COLS
    k0 = jnp.where(qc < Q_GROUPS[1][0], Q_GROUPS[0][2],
                   jnp.where(qc < Q_GROUPS[2][0], Q_GROUPS[1][2], Q_GROUPS[2][2]))
    kc = k0 + lane % KEY_COLS
    c_start = jnp.clip(qc - WIN_W // 2, 0, GRID_W - WIN_W)
    valid = (kc >= c_start) & (kc < c_start + WIN_W)

    def head_body(h, carry):
        def rolled(dr, j):
            row = ext_ref[0, pl.ds(h * N_BIAS_ROWS + dr, 1), :]
            parts = []
            for q0, nq, k0_g in Q_GROUPS:
                shift = (KEY_COLS * j - k0_g - (WIN_W - 1) - EXT_PAD + q0) % LANES
                parts.append(pltpu.roll(jnp.broadcast_to(row, (nq, LANES)), shift, 1,
                                        stride=1, stride_axis=0))
            return jnp.concatenate(parts, axis=0)

        for d0 in range(N_QUADS):
            quad = rolled(d0, 0)
            for j in range(1, LANES // KEY_COLS):
                quad = jnp.where(lane_j == j, rolled(d0 + j, j), quad)
            quad_ref[d0, pl.ds(pl.multiple_of(h * GRID_W, GRID_W), GRID_W), :] = (
                jnp.where(valid, quad * LOG2_E, MASK_BIAS))
        return carry

    lax.fori_loop(0, HEADS_PER_GROUP, head_body, 0)


def _attn_kernel(q_ref, k_ref, v_ref, ext_ref, o_ref, quad_ref,
                 s0_ref, s1_ref, e0_ref, e1_ref, l0_ref, l1_ref):
    rows = q_ref.shape[0] // GRID_W
    lane_head = lax.broadcasted_iota(jnp.int32, (1, GROUP_LANES), 1) // HEAD_DIM
    head_masks = [lane_head == h for h in range(HEADS_PER_GROUP)]

    @pl.when(pl.program_id(1) == 0)
    def _():
        _build_bias_quads(ext_ref, quad_ref)

    def key_rows(ref, r):
        r_start = jnp.clip(r - WIN_H // 2, 0, rows - WIN_H)
        return [ref[pl.ds(pl.multiple_of((r_start + j) * GRID_W, GRID_W), GRID_W), :]
                for j in range(WIN_H)]

    def window(slabs, k0):
        return jnp.concatenate([slab[k0:k0 + KEY_COLS] for slab in slabs], axis=0)

    def score_stage(r, s_ref):
        d0 = jnp.clip(r - WIN_H // 2, 0, rows - WIN_H) - r + (WIN_H - 1)
        q_row = q_ref[pl.ds(pl.multiple_of(r * GRID_W, GRID_W), GRID_W), :].astype(_F32)
        k_slabs = key_rows(k_ref, r)
        off = 0
        for q0, nq, k0 in Q_GROUPS:
            qg = q_row[q0:q0 + nq]
            qm = jnp.concatenate([jnp.where(mask, qg, 0.0) for mask in head_masks],
                                 axis=0).astype(_BF16)
            s = lax.dot_general(qm, window(k_slabs, k0), (((1,), (1,)), ((), ())),
                                preferred_element_type=_F32)
            bias = jnp.concatenate(
                [jnp.concatenate([quad_ref[d0 + half * (LANES // KEY_COLS),
                                           pl.ds(h * GRID_W + q0, nq), :]
                                  for h in range(HEADS_PER_GROUP)], axis=0)
                 for half in range(2)], axis=1)
            s_ref[pl.ds(off, HEADS_PER_GROUP * nq), :] = s + bias
            off += HEADS_PER_GROUP * nq

    def exp_stage(s_ref, e_ref, l_ref):
        off = 0
        for _, nq, _ in Q_GROUPS:
            rows_g = pl.ds(off, HEADS_PER_GROUP * nq)
            s = s_ref[rows_g, :]
            e = jnp.exp2(s - jnp.max(s, axis=-1, keepdims=True))
            e_ref[rows_g, :] = e.astype(e_ref.dtype)
            l_ref[rows_g, :] = jnp.broadcast_to(1.0 / jnp.sum(e, axis=-1, keepdims=True),
                                                (HEADS_PER_GROUP * nq, LANES))
            off += HEADS_PER_GROUP * nq

    def value_stage(r, e_ref, l_ref):
        outs = []
        v_slabs = key_rows(v_ref, r)
        off = 0
        for q0, nq, k0 in Q_GROUPS:
            rows_g = pl.ds(off, HEADS_PER_GROUP * nq)
            o_all = _dot(e_ref[rows_g, :], window(v_slabs, k0))
            inv = l_ref[rows_g, :]
            o_all = o_all * jnp.concatenate([inv, inv], axis=1)
            o = jnp.where(head_masks[0], o_all[:nq], 0.0)
            for h in range(1, HEADS_PER_GROUP):
                o = o + jnp.where(head_masks[h], o_all[h * nq:(h + 1) * nq], 0.0)
            outs.append(o)
            off += HEADS_PER_GROUP * nq
        o_ref[pl.ds(pl.multiple_of(r * GRID_W, GRID_W), GRID_W), :] = (
            jnp.concatenate(outs, axis=0).astype(o_ref.dtype))

    score_stage(0, s0_ref)
    score_stage(1, s1_ref)
    exp_stage(s0_ref, e0_ref, l0_ref)

    def row_pair(r):
        score_stage(r, s0_ref)
        exp_stage(s1_ref, e1_ref, l1_ref)
        value_stage(r - 2, e0_ref, l0_ref)
        score_stage(r + 1, s1_ref)
        exp_stage(s0_ref, e0_ref, l0_ref)
        value_stage(r - 1, e1_ref, l1_ref)

    def quad_body(k, carry):
        row_pair(4 * k + 2)
        row_pair(4 * k + 4)
        return carry

    lax.fori_loop(0, (rows - 4) // 4, quad_body, 0)
    row_pair(rows - 2)
    exp_stage(s1_ref, e1_ref, l1_ref)
    value_stage(rows - 2, e0_ref, l0_ref)
    value_stage(rows - 1, e1_ref, l1_ref)


def _attention_bias_ext(rpb):
    ext = jnp.pad(rpb, ((0, 0), (0, 0), (EXT_PAD, EXT_PAD)), mode="edge")
    ext = jnp.pad(ext, ((0, 0), (0, 0), (0, LANES - ext.shape[-1])))
    return ext.reshape(N_HEADS // HEADS_PER_GROUP, HEADS_PER_GROUP * N_BIAS_ROWS, LANES)


def neighbourhood_attention(qkv, rpb, batch, seq):
    n_hg = N_HEADS // HEADS_PER_GROUP
    rows = seq // GRID_W
    assert rows % 4 == 0 and rows >= WIN_H
    blk = (seq, GROUP_LANES)
    ext = _attention_bias_ext(rpb)
    score_scratch = pltpu.VMEM((GROUP_ROWS, WIN_H * KEY_COLS), _F32)
    exp_scratch = pltpu.VMEM((GROUP_ROWS, WIN_H * KEY_COLS), _BF16)
    inv_scratch = pltpu.VMEM((GROUP_ROWS, LANES), _F32)
    return pl.pallas_call(
        _attn_kernel,
        grid=(n_hg, batch),
        in_specs=[pl.BlockSpec(blk, lambda hg, b: (b, hg)),
                  pl.BlockSpec(blk, lambda hg, b: (b, n_hg + hg)),
                  pl.BlockSpec(blk, lambda hg, b: (b, 2 * n_hg + hg)),
                  pl.BlockSpec((1,) + ext.shape[1:], lambda hg, b: (hg, 0, 0))],
        out_specs=pl.BlockSpec(blk, lambda hg, b: (b, hg)),
        out_shape=jax.ShapeDtypeStruct((batch * seq, D_MODEL), _BF16),
        scratch_shapes=[pltpu.VMEM((N_QUADS, GROUP_ROWS, LANES), _F32),
                        score_scratch, score_scratch, exp_scratch, exp_scratch,
                        inv_scratch, inv_scratch],
        compiler_params=_params("arbitrary", "arbitrary"),
        name="nbr_attention",
    )(qkv, qkv, qkv, ext)


POOL_TM = 512
POOL_HALO = 8


def _pool_kernel(tiles_per_seq, x_ref, prev_ref, next_ref, g_ref, w_ref, scale_ref, gn_ref,
                 o_ref, xg_ref, ssq_ref, hext_ref, mixed_ref, wb_ref):
    i = pl.program_id(0)
    n_ext = POOL_TM + 2 * POOL_HALO
    inner = slice(POOL_HALO, POOL_HALO + POOL_TM)

    def ahead(v, s):
        return pltpu.roll(v, n_ext - s, 0)

    @pl.when(i == 0)
    def _():
        wb_ref[...] = w_ref[...].astype(wb_ref.dtype)

    def rms(v):
        return v * lax.rsqrt(jnp.mean(v * v, axis=-1, keepdims=True) + RMS_EPS) * g_ref[...]

    ti = i % tiles_per_seq
    seq = tiles_per_seq * POOL_TM
    x = x_ref[...]
    h = rms(x)
    has_prev = (ti > 0).astype(_F32)
    has_next = (ti < tiles_per_seq - 1).astype(_F32)
    hext_ref[pl.ds(0, POOL_HALO), :] = rms(prev_ref[...]) * has_prev
    hext_ref[pl.ds(POOL_HALO, POOL_TM), :] = h
    hext_ref[pl.ds(POOL_HALO + POOL_TM, POOL_HALO), :] = rms(next_ref[...]) * has_next
    t = ti * POOL_TM + lax.broadcasted_iota(jnp.int32, (POOL_TM, 1), 0)
    outs = []
    for gi, w in enumerate(POOL_WINDOWS):
        half = w // 2
        assert half <= POOL_HALO
        lo = jnp.clip(t - half, 0, seq)
        hi = jnp.clip(t + w - half, 0, seq)
        inv_cnt = 1.0 / (hi - lo).astype(_F32)
        for c in range(GROUP_CH // LANES):
            cols = slice(gi * GROUP_CH + c * LANES, gi * GROUP_CH + (c + 1) * LANES)
            hx = hext_ref[:, cols]
            run, span = hx, 1
            while span < half:
                run = run + ahead(run, span)
                span *= 2
            first = run if half == POOL_HALO else ahead(run, POOL_HALO - half)
            win = first[:POOL_TM] + run[inner]
            mixed_ref[:, cols] = (win * inv_cnt - hx[inner]).astype(mixed_ref.dtype)
        cols = slice(gi * GROUP_CH, (gi + 1) * GROUP_CH)
        outs.append(x[:, cols] + _dot(mixed_ref[:, cols], wb_ref[gi]) * scale_ref[:, cols])
    x_new = jnp.concatenate(outs, axis=1)
    o_ref[...] = x_new
    _emit_norm_inputs(x_new, gn_ref, xg_ref, ssq_ref)


def multiscale_pool(x, g, w_pool, layer, scale, g_next, seq):
    m, d = x.shape
    tiles_per_seq = seq // POOL_TM
    halo_blocks = POOL_TM // POOL_HALO
    last_halo = m // POOL_HALO - 1
    vec = pl.BlockSpec((1, d), lambda i: (0, 0))
    return pl.pallas_call(
        functools.partial(_pool_kernel, tiles_per_seq),
        grid=(m // POOL_TM,),
        in_specs=[pl.BlockSpec((POOL_TM, d), lambda i: (i, 0)),
                  pl.BlockSpec((POOL_HALO, d), lambda i: (jnp.maximum(i * halo_blocks - 1, 0), 0)),
                  pl.BlockSpec((POOL_HALO, d),
                               lambda i: (jnp.minimum((i + 1) * halo_blocks, last_halo), 0)),
                  vec,
                  pl.BlockSpec((None,) + w_pool.shape[1:], lambda i: (layer, 0, 0, 0)),
                  vec, vec],
        out_specs=[pl.BlockSpec((POOL_TM, d), lambda i: (i, 0)),
                   pl.BlockSpec((POOL_TM, d), lambda i: (i, 0)),
                   pl.BlockSpec((POOL_TM, LANES), lambda i: (i, 0))],
        out_shape=[jax.ShapeDtypeStruct((m, d), _F32), jax.ShapeDtypeStruct((m, d), _BF16),
                   jax.ShapeDtypeStruct((m, LANES), _F32)],
        scratch_shapes=[pltpu.VMEM((POOL_TM + 2 * POOL_HALO, d), _F32),
                        pltpu.VMEM((POOL_TM, d), _BF16),
                        pltpu.VMEM(w_pool.shape[1:], _BF16)],
        compiler_params=_params("arbitrary"),
        name="multiscale_pool",
    )(x, x, x, g.reshape(1, d), w_pool, scale.reshape(1, d), g_next.reshape(1, d))


def kernel(x, p, attn_norm_g, w_qkv, b_qkv, w_o, rpb, pool_norm_g, w_pool, pool_scale,
           ffn_norm_g, w_gate, w_up, w_down, ple_norm_g, w_ple_gate, b_ple_gate,
           w_ple_proj, final_norm_g):
    batch, seq, d = x.shape
    depth = p.shape[0]
    rows = seq // GRID_W
    assert d == D_MODEL and seq % GRID_W == 0 and rows >= WIN_H and seq % POOL_TM == 0
    m = batch * seq
    x = x.reshape(m, d)
    p = p.reshape(depth, m, p.shape[-1])
    q_scale = jnp.concatenate([jnp.full((d,), HEAD_DIM ** -0.5 * LOG2_E, _F32),
                               jnp.ones((2 * d,), _F32)])

    xg, ssq = norm_inputs(x, attn_norm_g[0])
    for i in range(depth):
        j = i // 2
        if i % 2 == 0:
            qkv = mm_qkv(xg, ssq, w_qkv, j, b_qkv[j], q_scale)
            o = neighbourhood_attention(qkv, rpb[j], batch, seq)
            x, xg, ssq = mm_residual(o, w_o, j, x, ffn_norm_g[i], tm=1024, tn=1024)
        else:
            x, xg, ssq = multiscale_pool(x, pool_norm_g[j], w_pool, j, pool_scale[j],
                                         ffn_norm_g[i], seq)
        hidden = mm_swiglu(xg, ssq, w_gate, w_up, i)
        x, xg, ssq = mm_residual(hidden, w_down, i, x, ple_norm_g[i])
        next_is_attn = i + 1 < depth and (i + 1) % 2 == 0
        g_next = attn_norm_g[(i + 1) // 2] if next_is_attn else None
        out = mm_ple(xg, ssq, w_ple_gate, b_ple_gate[i], p, w_ple_proj, i, x, g_next)
        x, xg, ssq = out if next_is_attn else (out, None, None)
    return rmsnorm(x, final_norm_g).reshape(batch, seq, d)
```

```python
import functools

import jax
import jax.numpy as jnp
import numpy as np
from jax import lax
from jax.experimental import pallas as pl
from jax.experimental.pallas import tpu as pltpu

D_MODEL = 2048
GRID_W = 64
N_HEADS = 64
HEAD_DIM = D_MODEL // N_HEADS
WIN_H = 8
WIN_W = 16
POOL_WINDOWS = (2, 4, 8, 16)
GROUP_CH = D_MODEL // len(POOL_WINDOWS)
RMS_EPS = 1e-6

VMEM_LIMIT_BYTES = 56 * 1024 * 1024
LANES = 128

HEADS_PER_GROUP = 8
GROUP_LANES = HEADS_PER_GROUP * HEAD_DIM
KEY_COLS = 32
Q_GROUPS = ((0, 24, 0), (24, 16, 16), (40, 24, 32))
for _q0, _nq, _k0 in Q_GROUPS:
    _cs = np.clip(np.arange(_q0, _q0 + _nq) - WIN_W // 2, 0, GRID_W - WIN_W)
    assert _cs.min() >= _k0 and _cs.max() + WIN_W <= _k0 + KEY_COLS
MASK_BIAS = -1e30
LOG2_E = float(np.log2(np.e))

_BF16 = jnp.bfloat16
_F32 = jnp.float32


def _params(*semantics):
    return pltpu.CompilerParams(dimension_semantics=semantics,
                                vmem_limit_bytes=VMEM_LIMIT_BYTES)


def _dot(a, b):
    return jnp.dot(a, b, preferred_element_type=_F32)


def _row_scale(ssq_ref, d):
    total = ssq_ref[:, 0:LANES]
    for part in range(1, ssq_ref.shape[1] // LANES):
        total = total + ssq_ref[:, part * LANES:(part + 1) * LANES]
    return lax.rsqrt(total[:, 0:1] * (1.0 / d) + RMS_EPS)


def _emit_norm_inputs(x, g_ref, xg_ref, ssq_ref):
    xg_ref[...] = (x * g_ref[...]).astype(xg_ref.dtype)
    ssq_ref[...] = jnp.broadcast_to(jnp.sum(x * x, axis=-1, keepdims=True), ssq_ref.shape)


def _prep_kernel(x_ref, g_ref, xg_ref, ssq_ref):
    _emit_norm_inputs(x_ref[...], g_ref, xg_ref, ssq_ref)


def norm_inputs(x, g, tm=512):
    m, d = x.shape
    return pl.pallas_call(
        _prep_kernel,
        grid=(m // tm,),
        in_specs=[pl.BlockSpec((tm, d), lambda i: (i, 0)),
                  pl.BlockSpec((1, d), lambda i: (0, 0))],
        out_specs=[pl.BlockSpec((tm, d), lambda i: (i, 0)),
                   pl.BlockSpec((tm, LANES), lambda i: (i, 0))],
        out_shape=[jax.ShapeDtypeStruct((m, d), _BF16),
                   jax.ShapeDtypeStruct((m, LANES), _F32)],
        compiler_params=_params("parallel"),
        name="norm_inputs",
    )(x, g.reshape(1, d))


CAST_ROWS = 256


def _cast_weight(w_ref, wb_ref):
    @pl.when(pl.program_id(1) == 0)
    def _():
        def body(c, carry):
            rows = pl.ds(pl.multiple_of(c * CAST_ROWS, CAST_ROWS), CAST_ROWS)
            wb_ref[rows, :] = w_ref[rows, :].astype(wb_ref.dtype)
            return carry
        lax.fori_loop(0, w_ref.shape[0] // CAST_ROWS, body, 0)


def _col_halves(ref):
    half = ref.shape[1] // 2
    return (slice(0, half), slice(half, 2 * half))


def _emit_norm_cols(x, cols, g_ref, xg_ref):
    xg_ref[:, cols] = (x * g_ref[:, cols]).astype(xg_ref.dtype)
    return jnp.sum(x * x, axis=-1, keepdims=True)


def _mm_qkv_kernel(d, xg_ref, ssq_ref, w_ref, b_ref, s_ref, o_ref, wb_ref):
    _cast_weight(w_ref, wb_ref)
    a = xg_ref[...]
    r = _row_scale(ssq_ref, d)
    for cols in _col_halves(o_ref):
        acc = _dot(a, wb_ref[:, cols]) * r
        o_ref[:, cols] = ((acc + b_ref[:, cols]) * s_ref[:, cols]).astype(o_ref.dtype)


def _mm_swiglu_kernel(d, xg_ref, ssq_ref, wg_ref, wu_ref, o_ref, wgb_ref, wub_ref):
    _cast_weight(wg_ref, wgb_ref)
    _cast_weight(wu_ref, wub_ref)
    a = xg_ref[...]
    r = _row_scale(ssq_ref, d)
    for cols in _col_halves(o_ref):
        gate = _dot(a, wgb_ref[:, cols]) * r
        up = _dot(a, wub_ref[:, cols]) * r
        o_ref[:, cols] = (gate * jax.nn.sigmoid(gate) * up).astype(o_ref.dtype)


def _mm_residual_kernel(a_ref, w_ref, res_ref, g_ref, x_ref, xg_ref, ssq_ref, wb_ref):
    _cast_weight(w_ref, wb_ref)
    a = a_ref[...]
    ssq = 0.0
    for cols in _col_halves(x_ref):
        x = res_ref[:, cols] + _dot(a, wb_ref[:, cols])
        x_ref[:, cols] = x
        ssq = ssq + _emit_norm_cols(x, cols, g_ref, xg_ref)
    ssq_ref[...] = jnp.broadcast_to(ssq, ssq_ref.shape)


def _mm_ple_kernel(d, mode, xg_ref, ssq_ref, wg_ref, b_ref, p_ref, wp_ref, res_ref, *rest):
    if mode == "emit":
        g_ref, x_ref, xg_out_ref, ssq_out_ref, wgb_ref, wpb_ref = rest
    elif mode == "final":
        g_ref, x_ref, wgb_ref, wpb_ref = rest
    else:
        x_ref, wgb_ref, wpb_ref = rest
    _cast_weight(wg_ref, wgb_ref)
    _cast_weight(wp_ref, wpb_ref)
    a = xg_ref[...]
    pb = p_ref[...].astype(_BF16)
    r = _row_scale(ssq_ref, d)
    ssq = 0.0
    for cols in _col_halves(x_ref):
        gate = jax.nn.sigmoid(_dot(a, wgb_ref[:, cols]) * r + b_ref[:, cols])
        x = res_ref[:, cols] + _dot(pb, wpb_ref[:, cols]) * gate
        x_ref[:, cols] = x
        if mode == "emit":
            ssq = ssq + _emit_norm_cols(x, cols, g_ref, xg_out_ref)
        elif mode == "final":
            ssq = ssq + jnp.sum(x * x, axis=-1, keepdims=True)
    if mode == "emit":
        ssq_out_ref[...] = jnp.broadcast_to(ssq, ssq_out_ref.shape)
    elif mode == "final":
        scale = lax.rsqrt(ssq * (1.0 / x_ref.shape[1]) + RMS_EPS)
        x_ref[...] = x_ref[...] * scale * g_ref[...]


def _row_spec(tm, k):
    return pl.BlockSpec((tm, k), lambda j, i: (i, 0))


def _col_spec(layer, k, tn, single_buffer=False):
    mode = pl.Buffered(1) if single_buffer else None
    return pl.BlockSpec((None, k, tn), lambda j, i: (layer, 0, j), pipeline_mode=mode)


def _vec_spec(tn):
    return pl.BlockSpec((1, tn), lambda j, i: (0, j))


def _tile_spec(tm, tn):
    return pl.BlockSpec((tm, tn), lambda j, i: (i, j))


def _ssq_out_spec(tm):
    return pl.BlockSpec((tm, LANES), lambda j, i: (i, j))


def _mm_params():
    return _params("parallel", "arbitrary")


def mm_qkv(xg, ssq, w, layer, b, s, tm=1024, tn=1024):
    m, k = xg.shape
    n = w.shape[-1]
    return pl.pallas_call(
        functools.partial(_mm_qkv_kernel, k),
        grid=(n // tn, m // tm),
        in_specs=[_row_spec(tm, k), _row_spec(tm, ssq.shape[1]), _col_spec(layer, k, tn),
                  _vec_spec(tn), _vec_spec(tn)],
        out_specs=_tile_spec(tm, tn),
        out_shape=jax.ShapeDtypeStruct((m, n), _BF16),
        scratch_shapes=[pltpu.VMEM((k, tn), _BF16)],
        compiler_params=_mm_params(),
        name="mm_qkv",
    )(xg, ssq, w, b.reshape(1, n), s.reshape(1, n))


def mm_swiglu(xg, ssq, wg, wu, layer, tm=2048, tn=512):
    m, k = xg.shape
    n = wg.shape[-1]
    return pl.pallas_call(
        functools.partial(_mm_swiglu_kernel, k),
        grid=(n // tn, m // tm),
        in_specs=[_row_spec(tm, k), _row_spec(tm, ssq.shape[1]),
                  _col_spec(layer, k, tn), _col_spec(layer, k, tn)],
        out_specs=_tile_spec(tm, tn),
        out_shape=jax.ShapeDtypeStruct((m, n), _BF16),
        scratch_shapes=[pltpu.VMEM((k, tn), _BF16), pltpu.VMEM((k, tn), _BF16)],
        compiler_params=_mm_params(),
        name="mm_swiglu",
    )(xg, ssq, wg, wu)


def _norm_out(m, n, tm, tn):
    specs = [_tile_spec(tm, tn), _tile_spec(tm, tn), _ssq_out_spec(tm)]
    shapes = [jax.ShapeDtypeStruct((m, n), _F32), jax.ShapeDtypeStruct((m, n), _BF16),
              jax.ShapeDtypeStruct((m, LANES * (n // tn)), _F32)]
    return specs, shapes


def mm_residual(a, w, layer, res, g_next, tm=1024, tn=512, single_buffer_weight=True):
    m, k = a.shape
    n = w.shape[-1]
    out_specs, out_shape = _norm_out(m, n, tm, tn)
    return pl.pallas_call(
        _mm_residual_kernel,
        grid=(n // tn, m // tm),
        in_specs=[_row_spec(tm, k), _col_spec(layer, k, tn, single_buffer_weight),
                  _tile_spec(tm, tn), _vec_spec(tn)],
        out_specs=out_specs,
        out_shape=out_shape,
        scratch_shapes=[pltpu.VMEM((k, tn), _BF16)],
        compiler_params=_mm_params(),
        name="mm_residual",
    )(a, w, res, g_next.reshape(1, n))


def mm_ple(xg, ssq, wg, b, p, wp, layer, res, g, mode, tm=1024, tn=1024):
    m, k = xg.shape
    n = wg.shape[-1]
    kp = p.shape[-1]
    if mode == "final":
        tm, tn = 512, n
    in_specs = [_row_spec(tm, k), _row_spec(tm, ssq.shape[1]),
                _col_spec(layer, k, tn, single_buffer=True), _vec_spec(tn),
                pl.BlockSpec((None, tm, kp), lambda j, i: (layer, i, 0)),
                _col_spec(layer, kp, tn, single_buffer=True), _tile_spec(tm, tn)]
    args = [xg, ssq, wg, b.reshape(1, n), p, wp, res]
    if mode != "plain":
        in_specs.append(_vec_spec(tn))
        args.append(g.reshape(1, n))
    if mode == "emit":
        out_specs, out_shape = _norm_out(m, n, tm, tn)
    else:
        out_specs, out_shape = _tile_spec(tm, tn), jax.ShapeDtypeStruct((m, n), _F32)
    return pl.pallas_call(
        functools.partial(_mm_ple_kernel, k, mode),
        grid=(n // tn, m // tm),
        in_specs=in_specs,
        out_specs=out_specs,
        out_shape=out_shape,
        scratch_shapes=[pltpu.VMEM((k, tn), _BF16), pltpu.VMEM((kp, tn), _BF16)],
        compiler_params=_mm_params(),
        name="mm_ple",
    )(*args)


EXT_PAD = KEY_COLS // 2
N_BIAS_ROWS = 2 * WIN_H - 1
N_QUADS = N_BIAS_ROWS - LANES // KEY_COLS + 1
GROUP_ROWS = HEADS_PER_GROUP * GRID_W


def _build_bias_quads(ext_ref, quad_ref):
    lane = lax.broadcasted_iota(jnp.int32, (GRID_W, LANES), 1)
    qc = lax.broadcasted_iota(jnp.int32, (GRID_W, LANES), 0)
    lane_j = lane // KEY_COLS
    k0 = jnp.where(qc < Q_GROUPS[1][0], Q_GROUPS[0][2],
                   jnp.where(qc < Q_GROUPS[2][0], Q_GROUPS[1][2], Q_GROUPS[2][2]))
    kc = k0 + lane % KEY_COLS
    c_start = jnp.clip(qc - WIN_W // 2, 0, GRID_W - WIN_W)
    valid = (kc >= c_start) & (kc < c_start + WIN_W)

    def head_body(h, carry):
        def rolled(dr, j):
            row = ext_ref[0, pl.ds(h * N_BIAS_ROWS + dr, 1), :]
            parts = []
            for q0, nq, k0_g in Q_GROUPS:
                shift = (KEY_COLS * j - k0_g - (WIN_W - 1) - EXT_PAD + q0) % LANES
                parts.append(pltpu.roll(jnp.broadcast_to(row, (nq, LANES)), shift, 1,
                                        stride=1, stride_axis=0))
            return jnp.concatenate(parts, axis=0)

        for d0 in range(N_QUADS):
            quad = rolled(d0, 0)
            for j in range(1, LANES // KEY_COLS):
                quad = jnp.where(lane_j == j, rolled(d0 + j, j), quad)
            quad_ref[d0, pl.ds(pl.multiple_of(h * GRID_W, GRID_W), GRID_W), :] = (
                jnp.where(valid, quad * LOG2_E, MASK_BIAS))
        return carry

    lax.fori_loop(0, HEADS_PER_GROUP, head_body, 0)


def _attn_kernel(q_ref, k_ref, v_ref, ext_ref, o_ref, quad_ref,
                 s0_ref, s1_ref, e0_ref, e1_ref, l0_ref, l1_ref):
    rows = q_ref.shape[0] // GRID_W
    lane_head = lax.broadcasted_iota(jnp.int32, (1, GROUP_LANES), 1) // HEAD_DIM
    head_masks = [lane_head == h for h in range(HEADS_PER_GROUP)]

    @pl.when(pl.program_id(1) == 0)
    def _():
        _build_bias_quads(ext_ref, quad_ref)

    def key_rows(ref, r):
        r_start = jnp.clip(r - WIN_H // 2, 0, rows - WIN_H)
        return [ref[pl.ds(pl.multiple_of((r_start + j) * GRID_W, GRID_W), GRID_W), :]
                for j in range(WIN_H)]

    def window(slabs, k0):
        return jnp.concatenate([slab[k0:k0 + KEY_COLS] for slab in slabs], axis=0)

    def score_stage(r, s_ref):
        d0 = jnp.clip(r - WIN_H // 2, 0, rows - WIN_H) - r + (WIN_H - 1)
        q_row = q_ref[pl.ds(pl.multiple_of(r * GRID_W, GRID_W), GRID_W), :].astype(_F32)
        k_slabs = key_rows(k_ref, r)
        off = 0
        for q0, nq, k0 in Q_GROUPS:
            qg = q_row[q0:q0 + nq]
            qm = jnp.concatenate([jnp.where(mask, qg, 0.0) for mask in head_masks],
                                 axis=0).astype(_BF16)
            s = lax.dot_general(qm, window(k_slabs, k0), (((1,), (1,)), ((), ())),
                                preferred_element_type=_F32)
            bias = jnp.concatenate(
                [jnp.concatenate([quad_ref[d0 + half * (LANES // KEY_COLS),
                                           pl.ds(h * GRID_W + q0, nq), :]
                                  for h in range(HEADS_PER_GROUP)], axis=0)
                 for half in range(2)], axis=1)
            s_ref[pl.ds(off, HEADS_PER_GROUP * nq), :] = s + bias
            off += HEADS_PER_GROUP * nq

    def exp_stage(s_ref, e_ref, l_ref):
        off = 0
        for _, nq, _ in Q_GROUPS:
            rows_g = pl.ds(off, HEADS_PER_GROUP * nq)
            s = s_ref[rows_g, :]
            e = jnp.exp2(s - jnp.max(s, axis=-1, keepdims=True))
            e_ref[rows_g, :] = e.astype(e_ref.dtype)
            l_ref[rows_g, :] = jnp.broadcast_to(1.0 / jnp.sum(e, axis=-1, keepdims=True),
                                                (HEADS_PER_GROUP * nq, LANES))
            off += HEADS_PER_GROUP * nq

    def value_stage(r, e_ref, l_ref):
        outs = []
        v_slabs = key_rows(v_ref, r)
        off = 0
        for q0, nq, k0 in Q_GROUPS:
            rows_g = pl.ds(off, HEADS_PER_GROUP * nq)
            o_all = _dot(e_ref[rows_g, :], window(v_slabs, k0))
            inv = l_ref[rows_g, :]
            o_all = o_all * jnp.concatenate([inv, inv], axis=1)
            o = jnp.where(head_masks[0], o_all[:nq], 0.0)
            for h in range(1, HEADS_PER_GROUP):
                o = o + jnp.where(head_masks[h], o_all[h * nq:(h + 1) * nq], 0.0)
            outs.append(o)
            off += HEADS_PER_GROUP * nq
        o_ref[pl.ds(pl.multiple_of(r * GRID_W, GRID_W), GRID_W), :] = (
            jnp.concatenate(outs, axis=0).astype(o_ref.dtype))

    score_stage(0, s0_ref)
    score_stage(1, s1_ref)
    exp_stage(s0_ref, e0_ref, l0_ref)

    def row_pair(r):
        score_stage(r, s0_ref)
        exp_stage(s1_ref, e1_ref, l1_ref)
        value_stage(r - 2, e0_ref, l0_ref)
        score_stage(r + 1, s1_ref)
        exp_stage(s0_ref, e0_ref, l0_ref)
        value_stage(r - 1, e1_ref, l1_ref)

    def quad_body(k, carry):
        row_pair(4 * k + 2)
        row_pair(4 * k + 4)
        return carry

    lax.fori_loop(0, (rows - 4) // 4, quad_body, 0)
    row_pair(rows - 2)
    exp_stage(s1_ref, e1_ref, l1_ref)
    value_stage(rows - 2, e0_ref, l0_ref)
    value_stage(rows - 1, e1_ref, l1_ref)


def _attention_bias_ext(rpb):
    ext = jnp.pad(rpb, ((0, 0), (0, 0), (EXT_PAD, EXT_PAD)), mode="edge")
    ext = jnp.pad(ext, ((0, 0), (0, 0), (0, LANES - ext.shape[-1])))
    return ext.reshape(N_HEADS // HEADS_PER_GROUP, HEADS_PER_GROUP * N_BIAS_ROWS, LANES)


def neighbourhood_attention(qkv, rpb, batch, seq):
    n_hg = N_HEADS // HEADS_PER_GROUP
    rows = seq // GRID_W
    assert rows % 4 == 0 and rows >= WIN_H
    blk = (seq, GROUP_LANES)
    ext = _attention_bias_ext(rpb)
    score_scratch = pltpu.VMEM((GROUP_ROWS, WIN_H * KEY_COLS), _F32)
    exp_scratch = pltpu.VMEM((GROUP_ROWS, WIN_H * KEY_COLS), _BF16)
    inv_scratch = pltpu.VMEM((GROUP_ROWS, LANES), _F32)
    return pl.pallas_call(
        _attn_kernel,
        grid=(n_hg, batch),
        in_specs=[pl.BlockSpec(blk, lambda hg, b: (b, hg)),
                  pl.BlockSpec(blk, lambda hg, b: (b, n_hg + hg)),
                  pl.BlockSpec(blk, lambda hg, b: (b, 2 * n_hg + hg)),
                  pl.BlockSpec((1,) + ext.shape[1:], lambda hg, b: (hg, 0, 0))],
        out_specs=pl.BlockSpec(blk, lambda hg, b: (b, hg)),
        out_shape=jax.ShapeDtypeStruct((batch * seq, D_MODEL), _BF16),
        scratch_shapes=[pltpu.VMEM((N_QUADS, GROUP_ROWS, LANES), _F32),
                        score_scratch, score_scratch, exp_scratch, exp_scratch,
                        inv_scratch, inv_scratch],
        compiler_params=_params("arbitrary", "arbitrary"),
        name="nbr_attention",
    )(qkv, qkv, qkv, ext)


POOL_TM = 512
POOL_HALO = 8


def _pool_kernel(tiles_per_seq, x_ref, prev_ref, next_ref, g_ref, w_ref, scale_ref, gn_ref,
                 o_ref, xg_ref, ssq_ref, hext_ref, mixed_ref, wb_ref):
    i = pl.program_id(0)
    n_ext = POOL_TM + 2 * POOL_HALO
    inner = slice(POOL_HALO, POOL_HALO + POOL_TM)

    def ahead(v, s):
        return pltpu.roll(v, n_ext - s, 0)

    @pl.when(i == 0)
    def _():
        wb_ref[...] = w_ref[...].astype(wb_ref.dtype)

    def rms(v):
        return v * lax.rsqrt(jnp.mean(v * v, axis=-1, keepdims=True) + RMS_EPS) * g_ref[...]

    ti = i % tiles_per_seq
    seq = tiles_per_seq * POOL_TM
    x = x_ref[...]
    h = rms(x)
    has_prev = (ti > 0).astype(_F32)
    has_next = (ti < tiles_per_seq - 1).astype(_F32)
    hext_ref[pl.ds(0, POOL_HALO), :] = rms(prev_ref[...]) * has_prev
    hext_ref[pl.ds(POOL_HALO, POOL_TM), :] = h
    hext_ref[pl.ds(POOL_HALO + POOL_TM, POOL_HALO), :] = rms(next_ref[...]) * has_next
    t = ti * POOL_TM + lax.broadcasted_iota(jnp.int32, (POOL_TM, 1), 0)
    outs = []
    for gi, w in enumerate(POOL_WINDOWS):
        half = w // 2
        assert half <= POOL_HALO
        lo = jnp.clip(t - half, 0, seq)
        hi = jnp.clip(t + w - half, 0, seq)
        inv_cnt = 1.0 / (hi - lo).astype(_F32)
        for c in range(GROUP_CH // LANES):
            cols = slice(gi * GROUP_CH + c * LANES, gi * GROUP_CH + (c + 1) * LANES)
            hx = hext_ref[:, cols]
            run, span = hx, 1
            while span < half:
                run = run + ahead(run, span)
                span *= 2
            first = run if half == POOL_HALO else ahead(run, POOL_HALO - half)
            win = first[:POOL_TM] + run[inner]
            mixed_ref[:, cols] = (win * inv_cnt - hx[inner]).astype(mixed_ref.dtype)
        cols = slice(gi * GROUP_CH, (gi + 1) * GROUP_CH)
        outs.append(x[:, cols] + _dot(mixed_ref[:, cols], wb_ref[gi]) * scale_ref[:, cols])
    x_new = jnp.concatenate(outs, axis=1)
    o_ref[...] = x_new
    _emit_norm_inputs(x_new, gn_ref, xg_ref, ssq_ref)


def multiscale_pool(x, g, w_pool, layer, scale, g_next, seq):
    m, d = x.shape
    tiles_per_seq = seq // POOL_TM
    halo_blocks = POOL_TM // POOL_HALO
    last_halo = m // POOL_HALO - 1
    vec = pl.BlockSpec((1, d), lambda i: (0, 0))
    return pl.pallas_call(
        functools.partial(_pool_kernel, tiles_per_seq),
        grid=(m // POOL_TM,),
        in_specs=[pl.BlockSpec((POOL_TM, d), lambda i: (i, 0)),
                  pl.BlockSpec((POOL_HALO, d), lambda i: (jnp.maximum(i * halo_blocks - 1, 0), 0)),
                  pl.BlockSpec((POOL_HALO, d),
                               lambda i: (jnp.minimum((i + 1) * halo_blocks, last_halo), 0)),
                  vec,
                  pl.BlockSpec((None,) + w_pool.shape[1:], lambda i: (layer, 0, 0, 0)),
                  vec, vec],
        out_specs=[pl.BlockSpec((POOL_TM, d), lambda i: (i, 0)),
                   pl.BlockSpec((POOL_TM, d), lambda i: (i, 0)),
                   pl.BlockSpec((POOL_TM, LANES), lambda i: (i, 0))],
        out_shape=[jax.ShapeDtypeStruct((m, d), _F32), jax.ShapeDtypeStruct((m, d), _BF16),
                   jax.ShapeDtypeStruct((m, LANES), _F32)],
        scratch_shapes=[pltpu.VMEM((POOL_TM + 2 * POOL_HALO, d), _F32),
                        pltpu.VMEM((POOL_TM, d), _BF16),
                        pltpu.VMEM(w_pool.shape[1:], _BF16)],
        compiler_params=_params("arbitrary"),
        name="multiscale_pool",
    )(x, x, x, g.reshape(1, d), w_pool, scale.reshape(1, d), g_next.reshape(1, d))


def kernel(x, p, attn_norm_g, w_qkv, b_qkv, w_o, rpb, pool_norm_g, w_pool, pool_scale,
           ffn_norm_g, w_gate, w_up, w_down, ple_norm_g, w_ple_gate, b_ple_gate,
           w_ple_proj, final_norm_g):
    batch, seq, d = x.shape
    depth = p.shape[0]
    rows = seq // GRID_W
    assert d == D_MODEL and seq % GRID_W == 0 and rows >= WIN_H and seq % POOL_TM == 0
    m = batch * seq
    x = x.reshape(m, d)
    p = p.reshape(depth, m, p.shape[-1])
    q_scale = jnp.concatenate([jnp.full((d,), HEAD_DIM ** -0.5 * LOG2_E, _F32),
                               jnp.ones((2 * d,), _F32)])

    xg, ssq = norm_inputs(x, attn_norm_g[0])
    for i in range(depth):
        j = i // 2
        if i % 2 == 0:
            qkv = mm_qkv(xg, ssq, w_qkv, j, b_qkv[j], q_scale)
            o = neighbourhood_attention(qkv, rpb[j], batch, seq)
            x, xg, ssq = mm_residual(o, w_o, j, x, ffn_norm_g[i], tn=1024, single_buffer_weight=False)
        else:
            x, xg, ssq = multiscale_pool(x, pool_norm_g[j], w_pool, j, pool_scale[j],
                                         ffn_norm_g[i], seq)
        hidden = mm_swiglu(xg, ssq, w_gate, w_up, i)
        x, xg, ssq = mm_residual(hidden, w_down, i, x, ple_norm_g[i])
        ple = functools.partial(mm_ple, xg, ssq, w_ple_gate, b_ple_gate[i], p, w_ple_proj, i, x)
        if i + 1 == depth:
            x = ple(final_norm_g, "final")
        elif (i + 1) % 2 == 0:
            x, xg, ssq = ple(attn_norm_g[(i + 1) // 2], "emit")
        else:
            x = ple(None, "plain")
    return x.reshape(batch, seq, d)
```

```python
import functools

import jax
import jax.numpy as jnp
import numpy as np
from jax import lax
from jax.experimental import pallas as pl
from jax.experimental.pallas import tpu as pltpu

D_MODEL = 2048
GRID_W = 64
N_HEADS = 64
HEAD_DIM = D_MODEL // N_HEADS
WIN_H = 8
WIN_W = 16
POOL_WINDOWS = (2, 4, 8, 16)
GROUP_CH = D_MODEL // len(POOL_WINDOWS)
RMS_EPS = 1e-6

VMEM_LIMIT_BYTES = 56 * 1024 * 1024
LANES = 128

HEADS_PER_GROUP = 8
GROUP_LANES = HEADS_PER_GROUP * HEAD_DIM
KEY_COLS = 32
Q_GROUPS = ((0, 24, 0), (24, 16, 16), (40, 24, 32))
for _q0, _nq, _k0 in Q_GROUPS:
    _cs = np.clip(np.arange(_q0, _q0 + _nq) - WIN_W // 2, 0, GRID_W - WIN_W)
    assert _cs.min() >= _k0 and _cs.max() + WIN_W <= _k0 + KEY_COLS
MASK_BIAS = -1e30
LOG2_E = float(np.log2(np.e))

_BF16 = jnp.bfloat16
_F32 = jnp.float32


def _params(*semantics):
    return pltpu.CompilerParams(dimension_semantics=semantics,
                                vmem_limit_bytes=VMEM_LIMIT_BYTES)


def _dot(a, b):
    return jnp.dot(a, b, preferred_element_type=_F32)


def _row_scale(ssq_ref, d):
    total = ssq_ref[:, 0:LANES]
    for part in range(1, ssq_ref.shape[1] // LANES):
        total = total + ssq_ref[:, part * LANES:(part + 1) * LANES]
    return lax.rsqrt(total[:, 0:1] * (1.0 / d) + RMS_EPS)


def _emit_norm_inputs(x, g_ref, xg_ref, ssq_ref):
    xg_ref[...] = (x * g_ref[...]).astype(xg_ref.dtype)
    ssq_ref[...] = jnp.broadcast_to(jnp.sum(x * x, axis=-1, keepdims=True), ssq_ref.shape)


def _prep_kernel(x_ref, g_ref, xg_ref, ssq_ref):
    _emit_norm_inputs(x_ref[...], g_ref, xg_ref, ssq_ref)


def norm_inputs(x, g, tm=512):
    m, d = x.shape
    return pl.pallas_call(
        _prep_kernel,
        grid=(m // tm,),
        in_specs=[pl.BlockSpec((tm, d), lambda i: (i, 0)),
                  pl.BlockSpec((1, d), lambda i: (0, 0))],
        out_specs=[pl.BlockSpec((tm, d), lambda i: (i, 0)),
                   pl.BlockSpec((tm, LANES), lambda i: (i, 0))],
        out_shape=[jax.ShapeDtypeStruct((m, d), _BF16),
                   jax.ShapeDtypeStruct((m, LANES), _F32)],
        compiler_params=_params("parallel"),
        name="norm_inputs",
    )(x, g.reshape(1, d))


CAST_ROWS = 256


def _cast_weight(w_ref, wb_ref):
    @pl.when(pl.program_id(1) == 0)
    def _():
        def body(c, carry):
            rows = pl.ds(pl.multiple_of(c * CAST_ROWS, CAST_ROWS), CAST_ROWS)
            wb_ref[rows, :] = w_ref[rows, :].astype(wb_ref.dtype)
            return carry
        lax.fori_loop(0, w_ref.shape[0] // CAST_ROWS, body, 0)


def _col_halves(ref):
    half = ref.shape[1] // 2
    return (slice(0, half), slice(half, 2 * half))


def _emit_norm_cols(x, cols, g_ref, xg_ref):
    xg_ref[:, cols] = (x * g_ref[:, cols]).astype(xg_ref.dtype)
    return jnp.sum(x * x, axis=-1, keepdims=True)


def _mm_qkv_kernel(d, xg_ref, ssq_ref, w_ref, b_ref, s_ref, o_ref, wb_ref):
    _cast_weight(w_ref, wb_ref)
    a = xg_ref[...]
    r = _row_scale(ssq_ref, d)
    for cols in _col_halves(o_ref):
        acc = _dot(a, wb_ref[:, cols]) * r
        o_ref[:, cols] = ((acc + b_ref[:, cols]) * s_ref[:, cols]).astype(o_ref.dtype)


def _mm_swiglu_kernel(d, xg_ref, ssq_ref, wg_ref, wu_ref, wd_ref, o_ref, wdb_ref, wgb_ref, wub_ref):
    _cast_weight(wg_ref, wgb_ref)
    _cast_weight(wu_ref, wub_ref)
    _cast_weight(wd_ref, wdb_ref)
    a = xg_ref[...]
    r = _row_scale(ssq_ref, d)
    for cols in _col_halves(o_ref):
        gate = _dot(a, wgb_ref[:, cols]) * r
        up = _dot(a, wub_ref[:, cols]) * r
        o_ref[:, cols] = (gate * jax.nn.sigmoid(gate) * up).astype(o_ref.dtype)


def _mm_residual_kernel(a_ref, w_ref, res_ref, g_ref, x_ref, xg_ref, ssq_ref, *scratch):
    if scratch:
        wb_ref, = scratch
        _cast_weight(w_ref, wb_ref)
    else:
        wb_ref = w_ref
    a = a_ref[...]
    ssq = 0.0
    for cols in _col_halves(x_ref):
        x = res_ref[:, cols] + _dot(a, wb_ref[:, cols])
        x_ref[:, cols] = x
        ssq = ssq + _emit_norm_cols(x, cols, g_ref, xg_ref)
    ssq_ref[...] = jnp.broadcast_to(ssq, ssq_ref.shape)


def _mm_ple_kernel(d, mode, xg_ref, ssq_ref, wg_ref, b_ref, p_ref, wp_ref, res_ref, *rest):
    if mode == "emit":
        g_ref, x_ref, xg_out_ref, ssq_out_ref, wgb_ref, wpb_ref = rest
    elif mode == "final":
        g_ref, x_ref, wgb_ref, wpb_ref = rest
    else:
        x_ref, wgb_ref, wpb_ref = rest
    _cast_weight(wg_ref, wgb_ref)
    _cast_weight(wp_ref, wpb_ref)
    a = xg_ref[...]
    pb = p_ref[...].astype(_BF16)
    r = _row_scale(ssq_ref, d)
    ssq = 0.0
    for cols in _col_halves(x_ref):
        gate = jax.nn.sigmoid(_dot(a, wgb_ref[:, cols]) * r + b_ref[:, cols])
        x = res_ref[:, cols] + _dot(pb, wpb_ref[:, cols]) * gate
        x_ref[:, cols] = x
        if mode == "emit":
            ssq = ssq + _emit_norm_cols(x, cols, g_ref, xg_out_ref)
        elif mode == "final":
            ssq = ssq + jnp.sum(x * x, axis=-1, keepdims=True)
    if mode == "emit":
        ssq_out_ref[...] = jnp.broadcast_to(ssq, ssq_out_ref.shape)
    elif mode == "final":
        scale = lax.rsqrt(ssq * (1.0 / x_ref.shape[1]) + RMS_EPS)
        x_ref[...] = x_ref[...] * scale * g_ref[...]


def _row_spec(tm, k):
    return pl.BlockSpec((tm, k), lambda j, i: (i, 0))


def _col_spec(layer, k, tn, single_buffer=False):
    mode = pl.Buffered(1) if single_buffer else None
    return pl.BlockSpec((None, k, tn), lambda j, i: (layer, 0, j), pipeline_mode=mode)


def _vec_spec(tn):
    return pl.BlockSpec((1, tn), lambda j, i: (0, j))


def _tile_spec(tm, tn):
    return pl.BlockSpec((tm, tn), lambda j, i: (i, j))


def _ssq_out_spec(tm):
    return pl.BlockSpec((tm, LANES), lambda j, i: (i, j))


def _mm_params():
    return _params("parallel", "arbitrary")


def mm_qkv(xg, ssq, w, layer, b, s, tm=1024, tn=1024):
    m, k = xg.shape
    n = w.shape[-1]
    return pl.pallas_call(
        functools.partial(_mm_qkv_kernel, k),
        grid=(n // tn, m // tm),
        in_specs=[_row_spec(tm, k), _row_spec(tm, ssq.shape[1]), _col_spec(layer, k, tn),
                  _vec_spec(tn), _vec_spec(tn)],
        out_specs=_tile_spec(tm, tn),
        out_shape=jax.ShapeDtypeStruct((m, n), _BF16),
        scratch_shapes=[pltpu.VMEM((k, tn), _BF16)],
        compiler_params=_mm_params(),
        name="mm_qkv",
    )(xg, ssq, w, b.reshape(1, n), s.reshape(1, n))


def mm_swiglu(xg, ssq, wg, wu, wd, layer, tm=1024, tn=512):
    m, k = xg.shape
    n = wg.shape[-1]
    nd = wd.shape[-1]
    return pl.pallas_call(
        functools.partial(_mm_swiglu_kernel, k),
        grid=(n // tn, m // tm),
        in_specs=[_row_spec(tm, k), _row_spec(tm, ssq.shape[1]),
                  _col_spec(layer, k, tn), _col_spec(layer, k, tn),
                  pl.BlockSpec((None, tn, nd), lambda j, i: (layer, j, 0))],
        out_specs=[_tile_spec(tm, tn), pl.BlockSpec((tn, nd), lambda j, i: (j, 0))],
        out_shape=[jax.ShapeDtypeStruct((m, n), _BF16), jax.ShapeDtypeStruct((n, nd), _BF16)],
        scratch_shapes=[pltpu.VMEM((k, tn), _BF16), pltpu.VMEM((k, tn), _BF16)],
        compiler_params=_mm_params(),
        name="mm_swiglu",
    )(xg, ssq, wg, wu, wd)


def _norm_out(m, n, tm, tn):
    specs = [_tile_spec(tm, tn), _tile_spec(tm, tn), _ssq_out_spec(tm)]
    shapes = [jax.ShapeDtypeStruct((m, n), _F32), jax.ShapeDtypeStruct((m, n), _BF16),
              jax.ShapeDtypeStruct((m, LANES * (n // tn)), _F32)]
    return specs, shapes


def mm_residual(a, w, layer, res, g_next, tm=512, tn=1024):
    m, k = a.shape
    n = w.shape[-1]
    out_specs, out_shape = _norm_out(m, n, tm, tn)
    if layer is None:
        w_spec, scratch = pl.BlockSpec((k, tn), lambda j, i: (0, j)), []
    else:
        w_spec, scratch = _col_spec(layer, k, tn), [pltpu.VMEM((k, tn), _BF16)]
    return pl.pallas_call(
        _mm_residual_kernel,
        grid=(n // tn, m // tm),
        in_specs=[_row_spec(tm, k), w_spec, _tile_spec(tm, tn), _vec_spec(tn)],
        out_specs=out_specs,
        out_shape=out_shape,
        scratch_shapes=scratch,
        compiler_params=_mm_params(),
        name="mm_residual",
    )(a, w, res, g_next.reshape(1, n))


def mm_ple(xg, ssq, wg, b, p, wp, layer, res, g, mode, tm=1024, tn=1024):
    m, k = xg.shape
    n = wg.shape[-1]
    kp = p.shape[-1]
    if mode == "final":
        tm, tn = 512, n
    in_specs = [_row_spec(tm, k), _row_spec(tm, ssq.shape[1]),
                _col_spec(layer, k, tn, single_buffer=True), _vec_spec(tn),
                pl.BlockSpec((None, tm, kp), lambda j, i: (layer, i, 0)),
                _col_spec(layer, kp, tn, single_buffer=True), _tile_spec(tm, tn)]
    args = [xg, ssq, wg, b.reshape(1, n), p, wp, res]
    if mode != "plain":
        in_specs.append(_vec_spec(tn))
        args.append(g.reshape(1, n))
    if mode == "emit":
        out_specs, out_shape = _norm_out(m, n, tm, tn)
    else:
        out_specs, out_shape = _tile_spec(tm, tn), jax.ShapeDtypeStruct((m, n), _F32)
    return pl.pallas_call(
        functools.partial(_mm_ple_kernel, k, mode),
        grid=(n // tn, m // tm),
        in_specs=in_specs,
        out_specs=out_specs,
        out_shape=out_shape,
        scratch_shapes=[pltpu.VMEM((k, tn), _BF16), pltpu.VMEM((kp, tn), _BF16)],
        compiler_params=_mm_params(),
        name="mm_ple",
    )(*args)


EXT_PAD = KEY_COLS // 2
N_BIAS_ROWS = 2 * WIN_H - 1
N_QUADS = N_BIAS_ROWS - LANES // KEY_COLS + 1
GROUP_ROWS = HEADS_PER_GROUP * GRID_W
PAIRS_PER_ITER = 4


def _build_bias_quads(ext_ref, quad_ref):
    lane = lax.broadcasted_iota(jnp.int32, (GRID_W, LANES), 1)
    qc = lax.broadcasted_iota(jnp.int32, (GRID_W, LANES), 0)
    lane_j = lane // KEY_COLS
    k0 = jnp.where(qc < Q_GROUPS[1][0], Q_GROUPS[0][2],
                   jnp.where(qc < Q_GROUPS[2][0], Q_GROUPS[1][2], Q_GROUPS[2][2]))
    kc = k0 + lane % KEY_COLS
    c_start = jnp.clip(qc - WIN_W // 2, 0, GRID_W - WIN_W)
    valid = (kc >= c_start) & (kc < c_start + WIN_W)

    def head_body(h, carry):
        def rolled(dr, j):
            row = ext_ref[0, pl.ds(h * N_BIAS_ROWS + dr, 1), :]
            parts = []
            for q0, nq, k0_g in Q_GROUPS:
                shift = (KEY_COLS * j - k0_g - (WIN_W - 1) - EXT_PAD + q0) % LANES
                parts.append(pltpu.roll(jnp.broadcast_to(row, (nq, LANES)), shift, 1,
                                        stride=1, stride_axis=0))
            return jnp.concatenate(parts, axis=0)

        for d0 in range(N_QUADS):
            quad = rolled(d0, 0)
            for j in range(1, LANES // KEY_COLS):
                quad = jnp.where(lane_j == j, rolled(d0 + j, j), quad)
            quad_ref[d0, pl.ds(pl.multiple_of(h * GRID_W, GRID_W), GRID_W), :] = (
                jnp.where(valid, quad * LOG2_E, MASK_BIAS))
        return carry

    lax.fori_loop(0, HEADS_PER_GROUP, head_body, 0)


def _attn_kernel(q_ref, k_ref, v_ref, ext_ref, o_ref, quad_ref,
                 s0_ref, s1_ref, e0_ref, e1_ref, l0_ref, l1_ref):
    rows = q_ref.shape[0] // GRID_W
    lane_head = lax.broadcasted_iota(jnp.int32, (1, GROUP_LANES), 1) // HEAD_DIM
    head_masks = [lane_head == h for h in range(HEADS_PER_GROUP)]

    @pl.when(pl.program_id(1) == 0)
    def _():
        _build_bias_quads(ext_ref, quad_ref)

    def key_rows(ref, r):
        r_start = jnp.clip(r - WIN_H // 2, 0, rows - WIN_H)
        return [ref[pl.ds(pl.multiple_of((r_start + j) * GRID_W, GRID_W), GRID_W), :]
                for j in range(WIN_H)]

    def window(slabs, k0):
        return jnp.concatenate([slab[k0:k0 + KEY_COLS] for slab in slabs], axis=0)

    def score_stage(r, s_ref):
        d0 = jnp.clip(r - WIN_H // 2, 0, rows - WIN_H) - r + (WIN_H - 1)
        q_row = q_ref[pl.ds(pl.multiple_of(r * GRID_W, GRID_W), GRID_W), :].astype(_F32)
        k_slabs = key_rows(k_ref, r)
        off = 0
        for q0, nq, k0 in Q_GROUPS:
            qg = q_row[q0:q0 + nq]
            qm = jnp.concatenate([jnp.where(mask, qg, 0.0) for mask in head_masks],
                                 axis=0).astype(_BF16)
            s = lax.dot_general(qm, window(k_slabs, k0), (((1,), (1,)), ((), ())),
                                preferred_element_type=_F32)
            bias = jnp.concatenate(
                [jnp.concatenate([quad_ref[d0 + half * (LANES // KEY_COLS),
                                           pl.ds(h * GRID_W + q0, nq), :]
                                  for h in range(HEADS_PER_GROUP)], axis=0)
                 for half in range(2)], axis=1)
            s_ref[pl.ds(off, HEADS_PER_GROUP * nq), :] = s + bias
            off += HEADS_PER_GROUP * nq

    def exp_stage(s_ref, e_ref, l_ref):
        off = 0
        for _, nq, _ in Q_GROUPS:
            rows_g = pl.ds(off, HEADS_PER_GROUP * nq)
            s = s_ref[rows_g, :]
            e = jnp.exp2(s - jnp.max(s, axis=-1, keepdims=True))
            e_ref[rows_g, :] = e.astype(e_ref.dtype)
            l_ref[rows_g, :] = jnp.broadcast_to(jnp.sum(e, axis=-1, keepdims=True),
                                                (HEADS_PER_GROUP * nq, LANES))
            off += HEADS_PER_GROUP * nq

    heads_per_vreg = LANES // HEAD_DIM
    half_masks = [head_masks[h][:, :LANES] for h in range(heads_per_vreg)]

    def value_stage(r, e_ref, l_ref):
        outs = []
        v_slabs = key_rows(v_ref, r)
        off = 0
        for q0, nq, k0 in Q_GROUPS:
            rows_g = pl.ds(off, HEADS_PER_GROUP * nq)
            o_all = _dot(e_ref[rows_g, :], window(v_slabs, k0))
            sums = l_ref[rows_g, :]
            o = jnp.where(head_masks[0], o_all[:nq], 0.0)
            for h in range(1, HEADS_PER_GROUP):
                o = o + jnp.where(head_masks[h], o_all[h * nq:(h + 1) * nq], 0.0)
            den = []
            for first in range(0, HEADS_PER_GROUP, heads_per_vreg):
                d = sums[first * nq:(first + 1) * nq]
                for h in range(1, heads_per_vreg):
                    d = jnp.where(half_masks[h], sums[(first + h) * nq:(first + h + 1) * nq], d)
                den.append(d)
            outs.append(o / jnp.concatenate(den, axis=1))
            off += HEADS_PER_GROUP * nq
        o_ref[pl.ds(pl.multiple_of(r * GRID_W, GRID_W), GRID_W), :] = (
            jnp.concatenate(outs, axis=0).astype(o_ref.dtype))

    score_stage(0, s0_ref)
    score_stage(1, s1_ref)
    exp_stage(s0_ref, e0_ref, l0_ref)

    def row_pair(r):
        score_stage(r, s0_ref)
        exp_stage(s1_ref, e1_ref, l1_ref)
        value_stage(r - 2, e0_ref, l0_ref)
        score_stage(r + 1, s1_ref)
        exp_stage(s0_ref, e0_ref, l0_ref)
        value_stage(r - 1, e1_ref, l1_ref)

    n_pairs = rows // 2 - 1
    n_iters = n_pairs // PAIRS_PER_ITER

    def loop_body(k, carry):
        for pair in range(PAIRS_PER_ITER):
            row_pair(2 * (PAIRS_PER_ITER * k + pair) + 2)
        return carry

    lax.fori_loop(0, n_iters, loop_body, 0)
    for pair in range(n_iters * PAIRS_PER_ITER, n_pairs):
        row_pair(2 * pair + 2)
    exp_stage(s1_ref, e1_ref, l1_ref)
    value_stage(rows - 2, e0_ref, l0_ref)
    value_stage(rows - 1, e1_ref, l1_ref)


def _attention_bias_ext(rpb):
    ext = jnp.pad(rpb, ((0, 0), (0, 0), (EXT_PAD, EXT_PAD)), mode="edge")
    ext = jnp.pad(ext, ((0, 0), (0, 0), (0, LANES - ext.shape[-1])))
    return ext.reshape(N_HEADS // HEADS_PER_GROUP, HEADS_PER_GROUP * N_BIAS_ROWS, LANES)


def neighbourhood_attention(qkv, rpb, batch, seq):
    n_hg = N_HEADS // HEADS_PER_GROUP
    rows = seq // GRID_W
    assert rows % 2 == 0 and rows >= WIN_H
    blk = (seq, GROUP_LANES)
    ext = _attention_bias_ext(rpb)
    score_scratch = pltpu.VMEM((GROUP_ROWS, WIN_H * KEY_COLS), _F32)
    exp_scratch = pltpu.VMEM((GROUP_ROWS, WIN_H * KEY_COLS), _BF16)
    inv_scratch = pltpu.VMEM((GROUP_ROWS, LANES), _F32)
    return pl.pallas_call(
        _attn_kernel,
        grid=(n_hg, batch),
        in_specs=[pl.BlockSpec(blk, lambda hg, b: (b, hg)),
                  pl.BlockSpec(blk, lambda hg, b: (b, n_hg + hg)),
                  pl.BlockSpec(blk, lambda hg, b: (b, 2 * n_hg + hg)),
                  pl.BlockSpec((1,) + ext.shape[1:], lambda hg, b: (hg, 0, 0))],
        out_specs=pl.BlockSpec(blk, lambda hg, b: (b, hg)),
        out_shape=jax.ShapeDtypeStruct((batch * seq, D_MODEL), _BF16),
        scratch_shapes=[pltpu.VMEM((N_QUADS, GROUP_ROWS, LANES), _F32),
                        score_scratch, score_scratch, exp_scratch, exp_scratch,
                        inv_scratch, inv_scratch],
        compiler_params=_params("arbitrary", "arbitrary"),
        name="nbr_attention",
    )(qkv, qkv, qkv, ext)


POOL_TM = 512
POOL_HALO = 8


def _pool_kernel(tiles_per_seq, x_ref, prev_ref, next_ref, g_ref, w_ref, scale_ref, gn_ref,
                 o_ref, xg_ref, ssq_ref, hext_ref, mixed_ref, wb_ref):
    i = pl.program_id(0)
    n_ext = POOL_TM + 2 * POOL_HALO
    inner = slice(POOL_HALO, POOL_HALO + POOL_TM)

    def ahead(v, s):
        return pltpu.roll(v, n_ext - s, 0)

    @pl.when(i == 0)
    def _():
        wb_ref[...] = w_ref[...].astype(wb_ref.dtype)

    def rms(v):
        return v * lax.rsqrt(jnp.mean(v * v, axis=-1, keepdims=True) + RMS_EPS) * g_ref[...]

    ti = i % tiles_per_seq
    seq = tiles_per_seq * POOL_TM
    x = x_ref[...]
    h = rms(x)
    has_prev = (ti > 0).astype(_F32)
    has_next = (ti < tiles_per_seq - 1).astype(_F32)
    hext_ref[pl.ds(0, POOL_HALO), :] = rms(prev_ref[...]) * has_prev
    hext_ref[pl.ds(POOL_HALO, POOL_TM), :] = h
    hext_ref[pl.ds(POOL_HALO + POOL_TM, POOL_HALO), :] = rms(next_ref[...]) * has_next
    t = ti * POOL_TM + lax.broadcasted_iota(jnp.int32, (POOL_TM, 1), 0)
    outs = []
    for gi, w in enumerate(POOL_WINDOWS):
        half = w // 2
        assert half <= POOL_HALO
        lo = jnp.clip(t - half, 0, seq)
        hi = jnp.clip(t + w - half, 0, seq)
        inv_cnt = 1.0 / (hi - lo).astype(_F32)
        for c in range(GROUP_CH // LANES):
            cols = slice(gi * GROUP_CH + c * LANES, gi * GROUP_CH + (c + 1) * LANES)
            hx = hext_ref[:, cols]
            run, span = hx, 1
            while span < half:
                run = run + ahead(run, span)
                span *= 2
            first = run if half == POOL_HALO else ahead(run, POOL_HALO - half)
            win = first[:POOL_TM] + run[inner]
            mixed_ref[:, cols] = (win * inv_cnt - hx[inner]).astype(mixed_ref.dtype)
        cols = slice(gi * GROUP_CH, (gi + 1) * GROUP_CH)
        outs.append(x[:, cols] + _dot(mixed_ref[:, cols], wb_ref[gi]) * scale_ref[:, cols])
    x_new = jnp.concatenate(outs, axis=1)
    o_ref[...] = x_new
    _emit_norm_inputs(x_new, gn_ref, xg_ref, ssq_ref)


def multiscale_pool(x, g, w_pool, layer, scale, g_next, seq):
    m, d = x.shape
    tiles_per_seq = seq // POOL_TM
    halo_blocks = POOL_TM // POOL_HALO
    last_halo = m // POOL_HALO - 1
    vec = pl.BlockSpec((1, d), lambda i: (0, 0))
    return pl.pallas_call(
        functools.partial(_pool_kernel, tiles_per_seq),
        grid=(m // POOL_TM,),
        in_specs=[pl.BlockSpec((POOL_TM, d), lambda i: (i, 0)),
                  pl.BlockSpec((POOL_HALO, d), lambda i: (jnp.maximum(i * halo_blocks - 1, 0), 0)),
                  pl.BlockSpec((POOL_HALO, d),
                               lambda i: (jnp.minimum((i + 1) * halo_blocks, last_halo), 0)),
                  vec,
                  pl.BlockSpec((None,) + w_pool.shape[1:], lambda i: (layer, 0, 0, 0)),
                  vec, vec],
        out_specs=[pl.BlockSpec((POOL_TM, d), lambda i: (i, 0)),
                   pl.BlockSpec((POOL_TM, d), lambda i: (i, 0)),
                   pl.BlockSpec((POOL_TM, LANES), lambda i: (i, 0))],
        out_shape=[jax.ShapeDtypeStruct((m, d), _F32), jax.ShapeDtypeStruct((m, d), _BF16),
                   jax.ShapeDtypeStruct((m, LANES), _F32)],
        scratch_shapes=[pltpu.VMEM((POOL_TM + 2 * POOL_HALO, d), _F32),
                        pltpu.VMEM((POOL_TM, d), _BF16),
                        pltpu.VMEM(w_pool.shape[1:], _BF16)],
        compiler_params=_params("arbitrary"),
        name="multiscale_pool",
    )(x, x, x, g.reshape(1, d), w_pool, scale.reshape(1, d), g_next.reshape(1, d))


def kernel(x, p, attn_norm_g, w_qkv, b_qkv, w_o, rpb, pool_norm_g, w_pool, pool_scale,
           ffn_norm_g, w_gate, w_up, w_down, ple_norm_g, w_ple_gate, b_ple_gate,
           w_ple_proj, final_norm_g):
    batch, seq, d = x.shape
    depth = p.shape[0]
    rows = seq // GRID_W
    assert d == D_MODEL and seq % GRID_W == 0 and rows >= WIN_H and seq % POOL_TM == 0
    m = batch * seq
    x = x.reshape(m, d)
    p = p.reshape(depth, m, p.shape[-1])
    q_scale = jnp.concatenate([jnp.full((d,), HEAD_DIM ** -0.5 * LOG2_E, _F32),
                               jnp.ones((2 * d,), _F32)])

    xg, ssq = norm_inputs(x, attn_norm_g[0])
    for i in range(depth):
        j = i // 2
        if i % 2 == 0:
            qkv = mm_qkv(xg, ssq, w_qkv, j, b_qkv[j], q_scale)
            o = neighbourhood_attention(qkv, rpb[j], batch, seq)
            x, xg, ssq = mm_residual(o, w_o, j, x, ffn_norm_g[i], tm=1024)
        else:
            x, xg, ssq = multiscale_pool(x, pool_norm_g[j], w_pool, j, pool_scale[j],
                                         ffn_norm_g[i], seq)
        hidden, w_down_bf16 = mm_swiglu(xg, ssq, w_gate, w_up, w_down, i)
        x, xg, ssq = mm_residual(hidden, w_down_bf16, None, x, ple_norm_g[i])
        ple = functools.partial(mm_ple, xg, ssq, w_ple_gate, b_ple_gate[i], p, w_ple_proj, i, x)
        if i + 1 == depth:
            x = ple(final_norm_g, "final")
        elif (i + 1) % 2 == 0:
            x, xg, ssq = ple(attn_norm_g[(i + 1) // 2], "emit")
        else:
            x = ple(None, "plain")
    return x.reshape(batch, seq, d)
```

```python
import functools

import jax
import jax.numpy as jnp
import numpy as np
from jax import lax
from jax.experimental import pallas as pl
from jax.experimental.pallas import tpu as pltpu

D_MODEL = 2048
GRID_W = 64
N_HEADS = 64
HEAD_DIM = D_MODEL // N_HEADS
WIN_H = 8
WIN_W = 16
POOL_WINDOWS = (2, 4, 8, 16)
GROUP_CH = D_MODEL // len(POOL_WINDOWS)
RMS_EPS = 1e-6

VMEM_LIMIT_BYTES = 56 * 1024 * 1024
LANES = 128

HEADS_PER_GROUP = 8
GROUP_LANES = HEADS_PER_GROUP * HEAD_DIM
KEY_COLS = 32
Q_GROUPS = ((0, 24, 0), (24, 16, 16), (40, 24, 32))
for _q0, _nq, _k0 in Q_GROUPS:
    _cs = np.clip(np.arange(_q0, _q0 + _nq) - WIN_W // 2, 0, GRID_W - WIN_W)
    assert _cs.min() >= _k0 and _cs.max() + WIN_W <= _k0 + KEY_COLS
MASK_BIAS = -1e30
LOG2_E = float(np.log2(np.e))

_BF16 = jnp.bfloat16
_F32 = jnp.float32


def _params(*semantics):
    return pltpu.CompilerParams(dimension_semantics=semantics,
                                vmem_limit_bytes=VMEM_LIMIT_BYTES)


def _dot(a, b):
    return jnp.dot(a, b, preferred_element_type=_F32)


def _row_scale(ssq_ref, d):
    total = ssq_ref[:, 0:LANES]
    for part in range(1, ssq_ref.shape[1] // LANES):
        total = total + ssq_ref[:, part * LANES:(part + 1) * LANES]
    return lax.rsqrt(total[:, 0:1] * (1.0 / d) + RMS_EPS)


def _emit_norm_inputs(x, g_ref, xg_ref, ssq_ref):
    xg_ref[...] = (x * g_ref[...]).astype(xg_ref.dtype)
    ssq_ref[...] = jnp.broadcast_to(jnp.sum(x * x, axis=-1, keepdims=True), ssq_ref.shape)


def _prep_kernel(x_ref, g_ref, xg_ref, ssq_ref):
    _emit_norm_inputs(x_ref[...], g_ref, xg_ref, ssq_ref)


def norm_inputs(x, g, tm=512):
    m, d = x.shape
    return pl.pallas_call(
        _prep_kernel,
        grid=(m // tm,),
        in_specs=[pl.BlockSpec((tm, d), lambda i: (i, 0)),
                  pl.BlockSpec((1, d), lambda i: (0, 0))],
        out_specs=[pl.BlockSpec((tm, d), lambda i: (i, 0)),
                   pl.BlockSpec((tm, LANES), lambda i: (i, 0))],
        out_shape=[jax.ShapeDtypeStruct((m, d), _BF16),
                   jax.ShapeDtypeStruct((m, LANES), _F32)],
        compiler_params=_params("parallel"),
        name="norm_inputs",
    )(x, g.reshape(1, d))


CAST_ROWS = 256


def _cast_weight(w_ref, wb_ref):
    @pl.when(pl.program_id(1) == 0)
    def _():
        def body(c, carry):
            rows = pl.ds(pl.multiple_of(c * CAST_ROWS, CAST_ROWS), CAST_ROWS)
            wb_ref[rows, :] = w_ref[rows, :].astype(wb_ref.dtype)
            return carry
        lax.fori_loop(0, w_ref.shape[0] // CAST_ROWS, body, 0)


def _col_halves(ref):
    half = ref.shape[1] // 2
    return (slice(0, half), slice(half, 2 * half))


def _emit_norm_cols(x, cols, g_ref, xg_ref):
    xg_ref[:, cols] = (x * g_ref[:, cols]).astype(xg_ref.dtype)
    return jnp.sum(x * x, axis=-1, keepdims=True)


def _mm_qkv_kernel(d, xg_ref, ssq_ref, w_ref, b_ref, s_ref, o_ref, wb_ref):
    _cast_weight(w_ref, wb_ref)
    a = xg_ref[...]
    r = _row_scale(ssq_ref, d)
    for cols in _col_halves(o_ref):
        acc = _dot(a, wb_ref[:, cols]) * r
        o_ref[:, cols] = ((acc + b_ref[:, cols]) * s_ref[:, cols]).astype(o_ref.dtype)


def _mm_swiglu_kernel(d, xg_ref, ssq_ref, wg_ref, wu_ref, o_ref, wgb_ref, wub_ref):
    _cast_weight(wg_ref, wgb_ref)
    _cast_weight(wu_ref, wub_ref)
    a = xg_ref[...]
    r = _row_scale(ssq_ref, d)
    for cols in _col_halves(o_ref):
        gate = _dot(a, wgb_ref[:, cols]) * r
        up = _dot(a, wub_ref[:, cols]) * r
        o_ref[:, cols] = (gate * jax.nn.sigmoid(gate) * up).astype(o_ref.dtype)


def _mm_residual_kernel(a_ref, w_ref, res_ref, g_ref, x_ref, xg_ref, ssq_ref, *scratch):
    if scratch:
        wb_ref, = scratch
        _cast_weight(w_ref, wb_ref)
    else:
        wb_ref = w_ref
    a = a_ref[...]
    ssq = 0.0
    for cols in _col_halves(x_ref):
        x = res_ref[:, cols] + _dot(a, wb_ref[:, cols])
        x_ref[:, cols] = x
        ssq = ssq + _emit_norm_cols(x, cols, g_ref, xg_ref)
    ssq_ref[...] = jnp.broadcast_to(ssq, ssq_ref.shape)


def _mm_ple_kernel(d, mode, xg_ref, ssq_ref, wg_ref, b_ref, p_ref, wp_ref, res_ref, *rest):
    if mode == "emit":
        g_ref, x_ref, xg_out_ref, ssq_out_ref, wgb_ref, wpb_ref = rest
    elif mode == "final":
        g_ref, x_ref, wgb_ref, wpb_ref = rest
    else:
        x_ref, wgb_ref, wpb_ref = rest
    _cast_weight(wg_ref, wgb_ref)
    _cast_weight(wp_ref, wpb_ref)
    a = xg_ref[...]
    pb = p_ref[...].astype(_BF16)
    r = _row_scale(ssq_ref, d)
    ssq = 0.0
    for cols in _col_halves(x_ref):
        gate = jax.nn.sigmoid(_dot(a, wgb_ref[:, cols]) * r + b_ref[:, cols])
        x = res_ref[:, cols] + _dot(pb, wpb_ref[:, cols]) * gate
        x_ref[:, cols] = x
        if mode == "emit":
            ssq = ssq + _emit_norm_cols(x, cols, g_ref, xg_out_ref)
        elif mode == "final":
            ssq = ssq + jnp.sum(x * x, axis=-1, keepdims=True)
    if mode == "emit":
        ssq_out_ref[...] = jnp.broadcast_to(ssq, ssq_out_ref.shape)
    elif mode == "final":
        scale = lax.rsqrt(ssq * (1.0 / x_ref.shape[1]) + RMS_EPS)
        x_ref[...] = x_ref[...] * scale * g_ref[...]


def _row_spec(tm, k):
    return pl.BlockSpec((tm, k), lambda j, i: (i, 0))


def _col_spec(layer, k, tn, single_buffer=False):
    mode = pl.Buffered(1) if single_buffer else None
    return pl.BlockSpec((None, k, tn), lambda j, i: (layer, 0, j), pipeline_mode=mode)


def _vec_spec(tn):
    return pl.BlockSpec((1, tn), lambda j, i: (0, j))


def _tile_spec(tm, tn):
    return pl.BlockSpec((tm, tn), lambda j, i: (i, j))


def _ssq_out_spec(tm):
    return pl.BlockSpec((tm, LANES), lambda j, i: (i, j))


def _mm_params():
    return _params("parallel", "arbitrary")


def mm_qkv(xg, ssq, w, layer, b, s, tm=1024, tn=1024):
    m, k = xg.shape
    n = w.shape[-1]
    return pl.pallas_call(
        functools.partial(_mm_qkv_kernel, k),
        grid=(n // tn, m // tm),
        in_specs=[_row_spec(tm, k), _row_spec(tm, ssq.shape[1]), _col_spec(layer, k, tn),
                  _vec_spec(tn), _vec_spec(tn)],
        out_specs=_tile_spec(tm, tn),
        out_shape=jax.ShapeDtypeStruct((m, n), _BF16),
        scratch_shapes=[pltpu.VMEM((k, tn), _BF16)],
        compiler_params=_mm_params(),
        name="mm_qkv",
    )(xg, ssq, w, b.reshape(1, n), s.reshape(1, n))


def mm_swiglu(xg, ssq, wg, wu, layer, tm=2048, tn=512):
    m, k = xg.shape
    n = wg.shape[-1]
    return pl.pallas_call(
        functools.partial(_mm_swiglu_kernel, k),
        grid=(n // tn, m // tm),
        in_specs=[_row_spec(tm, k), _row_spec(tm, ssq.shape[1]),
                  _col_spec(layer, k, tn), _col_spec(layer, k, tn)],
        out_specs=_tile_spec(tm, tn),
        out_shape=jax.ShapeDtypeStruct((m, n), _BF16),
        scratch_shapes=[pltpu.VMEM((k, tn), _BF16), pltpu.VMEM((k, tn), _BF16)],
        compiler_params=_mm_params(),
        name="mm_swiglu",
    )(xg, ssq, wg, wu)


def _norm_out(m, n, tm, tn):
    specs = [_tile_spec(tm, tn), _tile_spec(tm, tn), _ssq_out_spec(tm)]
    shapes = [jax.ShapeDtypeStruct((m, n), _F32), jax.ShapeDtypeStruct((m, n), _BF16),
              jax.ShapeDtypeStruct((m, LANES * (n // tn)), _F32)]
    return specs, shapes


def mm_residual(a, w, layer, res, g_next, tm=512, tn=1024):
    m, k = a.shape
    n = w.shape[-1]
    out_specs, out_shape = _norm_out(m, n, tm, tn)
    if layer is None:
        w_spec, scratch = pl.BlockSpec((k, tn), lambda j, i: (0, j)), []
    else:
        w_spec, scratch = _col_spec(layer, k, tn), [pltpu.VMEM((k, tn), _BF16)]
    return pl.pallas_call(
        _mm_residual_kernel,
        grid=(n // tn, m // tm),
        in_specs=[_row_spec(tm, k), w_spec, _tile_spec(tm, tn), _vec_spec(tn)],
        out_specs=out_specs,
        out_shape=out_shape,
        scratch_shapes=scratch,
        compiler_params=_mm_params(),
        name="mm_residual",
    )(a, w, res, g_next.reshape(1, n))


def mm_ple(xg, ssq, wg, b, p, wp, layer, res, g, mode, tm=1024, tn=1024):
    m, k = xg.shape
    n = wg.shape[-1]
    kp = p.shape[-1]
    if mode == "final":
        tm, tn = 512, n
    in_specs = [_row_spec(tm, k), _row_spec(tm, ssq.shape[1]),
                _col_spec(layer, k, tn, single_buffer=True), _vec_spec(tn),
                pl.BlockSpec((None, tm, kp), lambda j, i: (layer, i, 0)),
                _col_spec(layer, kp, tn, single_buffer=True), _tile_spec(tm, tn)]
    args = [xg, ssq, wg, b.reshape(1, n), p, wp, res]
    if mode != "plain":
        in_specs.append(_vec_spec(tn))
        args.append(g.reshape(1, n))
    if mode == "emit":
        out_specs, out_shape = _norm_out(m, n, tm, tn)
    else:
        out_specs, out_shape = _tile_spec(tm, tn), jax.ShapeDtypeStruct((m, n), _F32)
    return pl.pallas_call(
        functools.partial(_mm_ple_kernel, k, mode),
        grid=(n // tn, m // tm),
        in_specs=in_specs,
        out_specs=out_specs,
        out_shape=out_shape,
        scratch_shapes=[pltpu.VMEM((k, tn), _BF16), pltpu.VMEM((kp, tn), _BF16)],
        compiler_params=_mm_params(),
        name="mm_ple",
    )(*args)


EXT_PAD = KEY_COLS // 2
N_BIAS_ROWS = 2 * WIN_H - 1
N_QUADS = N_BIAS_ROWS - LANES // KEY_COLS + 1
GROUP_ROWS = HEADS_PER_GROUP * GRID_W
PAIRS_PER_ITER = 4


def _build_bias_quads(ext_ref, quad_ref):
    lane = lax.broadcasted_iota(jnp.int32, (GRID_W, LANES), 1)
    qc = lax.broadcasted_iota(jnp.int32, (GRID_W, LANES), 0)
    lane_j = lane // KEY_COLS
    k0 = jnp.where(qc < Q_GROUPS[1][0], Q_GROUPS[0][2],
                   jnp.where(qc < Q_GROUPS[2][0], Q_GROUPS[1][2], Q_GROUPS[2][2]))
    kc = k0 + lane % KEY_COLS
    c_start = jnp.clip(qc - WIN_W // 2, 0, GRID_W - WIN_W)
    valid = (kc >= c_start) & (kc < c_start + WIN_W)

    def head_body(h, carry):
        def rolled(dr, j):
            row = ext_ref[0, pl.ds(h * N_BIAS_ROWS + dr, 1), :]
            parts = []
            for q0, nq, k0_g in Q_GROUPS:
                shift = (KEY_COLS * j - k0_g - (WIN_W - 1) - EXT_PAD + q0) % LANES
                parts.append(pltpu.roll(jnp.broadcast_to(row, (nq, LANES)), shift, 1,
                                        stride=1, stride_axis=0))
            return jnp.concatenate(parts, axis=0)

        for d0 in range(N_QUADS):
            quad = rolled(d0, 0)
            for j in range(1, LANES // KEY_COLS):
                quad = jnp.where(lane_j == j, rolled(d0 + j, j), quad)
            quad_ref[d0, pl.ds(pl.multiple_of(h * GRID_W, GRID_W), GRID_W), :] = (
                jnp.where(valid, quad * LOG2_E, MASK_BIAS))
        return carry

    lax.fori_loop(0, HEADS_PER_GROUP, head_body, 0)


def _attn_kernel(q_ref, k_ref, v_ref, ext_ref, wd_ref, o_ref, wdb_ref, quad_ref,
                 s0_ref, s1_ref, e0_ref, e1_ref, l0_ref, l1_ref):
    rows = q_ref.shape[0] // GRID_W
    wdb_ref[...] = wd_ref[...].astype(wdb_ref.dtype)
    lane_head = lax.broadcasted_iota(jnp.int32, (1, GROUP_LANES), 1) // HEAD_DIM
    head_masks = [lane_head == h for h in range(HEADS_PER_GROUP)]

    @pl.when(pl.program_id(1) == 0)
    def _():
        _build_bias_quads(ext_ref, quad_ref)

    def key_rows(ref, r):
        r_start = jnp.clip(r - WIN_H // 2, 0, rows - WIN_H)
        return [ref[pl.ds(pl.multiple_of((r_start + j) * GRID_W, GRID_W), GRID_W), :]
                for j in range(WIN_H)]

    def window(slabs, k0):
        return jnp.concatenate([slab[k0:k0 + KEY_COLS] for slab in slabs], axis=0)

    def score_stage(r, s_ref):
        d0 = jnp.clip(r - WIN_H // 2, 0, rows - WIN_H) - r + (WIN_H - 1)
        q_row = q_ref[pl.ds(pl.multiple_of(r * GRID_W, GRID_W), GRID_W), :].astype(_F32)
        k_slabs = key_rows(k_ref, r)
        off = 0
        for q0, nq, k0 in Q_GROUPS:
            qg = q_row[q0:q0 + nq]
            qm = jnp.concatenate([jnp.where(mask, qg, 0.0) for mask in head_masks],
                                 axis=0).astype(_BF16)
            s = lax.dot_general(qm, window(k_slabs, k0), (((1,), (1,)), ((), ())),
                                preferred_element_type=_F32)
            bias = jnp.concatenate(
                [jnp.concatenate([quad_ref[d0 + half * (LANES // KEY_COLS),
                                           pl.ds(h * GRID_W + q0, nq), :]
                                  for h in range(HEADS_PER_GROUP)], axis=0)
                 for half in range(2)], axis=1)
            s_ref[pl.ds(off, HEADS_PER_GROUP * nq), :] = s + bias
            off += HEADS_PER_GROUP * nq

    def exp_stage(s_ref, e_ref, l_ref):
        off = 0
        for _, nq, _ in Q_GROUPS:
            rows_g = pl.ds(off, HEADS_PER_GROUP * nq)
            s = s_ref[rows_g, :]
            e = jnp.exp2(s - jnp.max(s, axis=-1, keepdims=True))
            e_ref[rows_g, :] = e.astype(e_ref.dtype)
            l_ref[rows_g, :] = jnp.broadcast_to(jnp.sum(e, axis=-1, keepdims=True),
                                                (HEADS_PER_GROUP * nq, LANES))
            off += HEADS_PER_GROUP * nq

    heads_per_vreg = LANES // HEAD_DIM
    half_masks = [head_masks[h][:, :LANES] for h in range(heads_per_vreg)]

    def value_stage(r, e_ref, l_ref):
        outs = []
        v_slabs = key_rows(v_ref, r)
        off = 0
        for q0, nq, k0 in Q_GROUPS:
            rows_g = pl.ds(off, HEADS_PER_GROUP * nq)
            o_all = _dot(e_ref[rows_g, :], window(v_slabs, k0))
            sums = l_ref[rows_g, :]
            o = jnp.where(head_masks[0], o_all[:nq], 0.0)
            for h in range(1, HEADS_PER_GROUP):
                o = o + jnp.where(head_masks[h], o_all[h * nq:(h + 1) * nq], 0.0)
            den = []
            for first in range(0, HEADS_PER_GROUP, heads_per_vreg):
                d = sums[first * nq:(first + 1) * nq]
                for h in range(1, heads_per_vreg):
                    d = jnp.where(half_masks[h], sums[(first + h) * nq:(first + h + 1) * nq], d)
                den.append(d)
            outs.append(o / jnp.concatenate(den, axis=1))
            off += HEADS_PER_GROUP * nq
        o_ref[pl.ds(pl.multiple_of(r * GRID_W, GRID_W), GRID_W), :] = (
            jnp.concatenate(outs, axis=0).astype(o_ref.dtype))

    score_stage(0, s0_ref)
    score_stage(1, s1_ref)
    exp_stage(s0_ref, e0_ref, l0_ref)

    def row_pair(r):
        score_stage(r, s0_ref)
        exp_stage(s1_ref, e1_ref, l1_ref)
        value_stage(r - 2, e0_ref, l0_ref)
        score_stage(r + 1, s1_ref)
        exp_stage(s0_ref, e0_ref, l0_ref)
        value_stage(r - 1, e1_ref, l1_ref)

    n_pairs = rows // 2 - 1
    n_iters = n_pairs // PAIRS_PER_ITER

    def loop_body(k, carry):
        for pair in range(PAIRS_PER_ITER):
            row_pair(2 * (PAIRS_PER_ITER * k + pair) + 2)
        return carry

    lax.fori_loop(0, n_iters, loop_body, 0)
    for pair in range(n_iters * PAIRS_PER_ITER, n_pairs):
        row_pair(2 * pair + 2)
    exp_stage(s1_ref, e1_ref, l1_ref)
    value_stage(rows - 2, e0_ref, l0_ref)
    value_stage(rows - 1, e1_ref, l1_ref)


def _attention_bias_ext(rpb):
    ext = jnp.pad(rpb, ((0, 0), (0, 0), (EXT_PAD, EXT_PAD)), mode="edge")
    ext = jnp.pad(ext, ((0, 0), (0, 0), (0, LANES - ext.shape[-1])))
    return ext.reshape(N_HEADS // HEADS_PER_GROUP, HEADS_PER_GROUP * N_BIAS_ROWS, LANES)


def _cast_share(w, layer, steps):
    k, n = w.shape[1:]
    assert k % (steps * 16) == 0
    return (k // steps, n), jax.ShapeDtypeStruct((k, n), _BF16)


def neighbourhood_attention(qkv, rpb, w_side, layer, batch, seq):
    n_hg = N_HEADS // HEADS_PER_GROUP
    side_blk, side_shape = _cast_share(w_side, layer, n_hg * batch)
    rows = seq // GRID_W
    assert rows % 2 == 0 and rows >= WIN_H
    blk = (seq, GROUP_LANES)
    ext = _attention_bias_ext(rpb)
    score_scratch = pltpu.VMEM((GROUP_ROWS, WIN_H * KEY_COLS), _F32)
    exp_scratch = pltpu.VMEM((GROUP_ROWS, WIN_H * KEY_COLS), _BF16)
    inv_scratch = pltpu.VMEM((GROUP_ROWS, LANES), _F32)
    return pl.pallas_call(
        _attn_kernel,
        grid=(n_hg, batch),
        in_specs=[pl.BlockSpec(blk, lambda hg, b: (b, hg)),
                  pl.BlockSpec(blk, lambda hg, b: (b, n_hg + hg)),
                  pl.BlockSpec(blk, lambda hg, b: (b, 2 * n_hg + hg)),
                  pl.BlockSpec((1,) + ext.shape[1:], lambda hg, b: (hg, 0, 0)),
                  pl.BlockSpec((None,) + side_blk, lambda hg, b: (layer, hg * batch + b, 0))],
        out_specs=[pl.BlockSpec(blk, lambda hg, b: (b, hg)),
                   pl.BlockSpec(side_blk, lambda hg, b: (hg * batch + b, 0))],
        out_shape=[jax.ShapeDtypeStruct((batch * seq, D_MODEL), _BF16), side_shape],
        scratch_shapes=[pltpu.VMEM((N_QUADS, GROUP_ROWS, LANES), _F32),
                        score_scratch, score_scratch, exp_scratch, exp_scratch,
                        inv_scratch, inv_scratch],
        compiler_params=_params("arbitrary", "arbitrary"),
        name="nbr_attention",
    )(qkv, qkv, qkv, ext, w_side)


POOL_TM = 512
POOL_HALO = 8


def _pool_kernel(tiles_per_seq, x_ref, prev_ref, next_ref, g_ref, w_ref, scale_ref, gn_ref, wd_ref,
                 o_ref, xg_ref, ssq_ref, wdb_ref, hext_ref, mixed_ref, wb_ref):
    i = pl.program_id(0)
    wdb_ref[...] = wd_ref[...].astype(wdb_ref.dtype)
    n_ext = POOL_TM + 2 * POOL_HALO
    inner = slice(POOL_HALO, POOL_HALO + POOL_TM)

    def ahead(v, s):
        return pltpu.roll(v, n_ext - s, 0)

    @pl.when(i == 0)
    def _():
        wb_ref[...] = w_ref[...].astype(wb_ref.dtype)

    def rms(v):
        return v * lax.rsqrt(jnp.mean(v * v, axis=-1, keepdims=True) + RMS_EPS) * g_ref[...]

    ti = i % tiles_per_seq
    seq = tiles_per_seq * POOL_TM
    x = x_ref[...]
    h = rms(x)
    has_prev = (ti > 0).astype(_F32)
    has_next = (ti < tiles_per_seq - 1).astype(_F32)
    hext_ref[pl.ds(0, POOL_HALO), :] = rms(prev_ref[...]) * has_prev
    hext_ref[pl.ds(POOL_HALO, POOL_TM), :] = h
    hext_ref[pl.ds(POOL_HALO + POOL_TM, POOL_HALO), :] = rms(next_ref[...]) * has_next
    t = ti * POOL_TM + lax.broadcasted_iota(jnp.int32, (POOL_TM, 1), 0)
    outs = []
    for gi, w in enumerate(POOL_WINDOWS):
        half = w // 2
        assert half <= POOL_HALO
        lo = jnp.clip(t - half, 0, seq)
        hi = jnp.clip(t + w - half, 0, seq)
        inv_cnt = 1.0 / (hi - lo).astype(_F32)
        for c in range(GROUP_CH // LANES):
            cols = slice(gi * GROUP_CH + c * LANES, gi * GROUP_CH + (c + 1) * LANES)
            hx = hext_ref[:, cols]
            run, span = hx, 1
            while span < half:
                run = run + ahead(run, span)
                span *= 2
            first = run if half == POOL_HALO else ahead(run, POOL_HALO - half)
            win = first[:POOL_TM] + run[inner]
            mixed_ref[:, cols] = (win * inv_cnt - hx[inner]).astype(mixed_ref.dtype)
        cols = slice(gi * GROUP_CH, (gi + 1) * GROUP_CH)
        outs.append(x[:, cols] + _dot(mixed_ref[:, cols], wb_ref[gi]) * scale_ref[:, cols])
    x_new = jnp.concatenate(outs, axis=1)
    o_ref[...] = x_new
    _emit_norm_inputs(x_new, gn_ref, xg_ref, ssq_ref)


def multiscale_pool(x, g, w_pool, layer, scale, g_next, w_side, side_layer, seq):
    m, d = x.shape
    side_blk, side_shape = _cast_share(w_side, side_layer, m // POOL_TM)
    tiles_per_seq = seq // POOL_TM
    halo_blocks = POOL_TM // POOL_HALO
    last_halo = m // POOL_HALO - 1
    vec = pl.BlockSpec((1, d), lambda i: (0, 0))
    return pl.pallas_call(
        functools.partial(_pool_kernel, tiles_per_seq),
        grid=(m // POOL_TM,),
        in_specs=[pl.BlockSpec((POOL_TM, d), lambda i: (i, 0)),
                  pl.BlockSpec((POOL_HALO, d), lambda i: (jnp.maximum(i * halo_blocks - 1, 0), 0)),
                  pl.BlockSpec((POOL_HALO, d),
                               lambda i: (jnp.minimum((i + 1) * halo_blocks, last_halo), 0)),
                  vec,
                  pl.BlockSpec((None,) + w_pool.shape[1:], lambda i: (layer, 0, 0, 0)),
                  vec, vec,
                  pl.BlockSpec((None,) + side_blk, lambda i: (side_layer, i, 0))],
        out_specs=[pl.BlockSpec((POOL_TM, d), lambda i: (i, 0)),
                   pl.BlockSpec((POOL_TM, d), lambda i: (i, 0)),
                   pl.BlockSpec((POOL_TM, LANES), lambda i: (i, 0)),
                   pl.BlockSpec(side_blk, lambda i: (i, 0))],
        out_shape=[jax.ShapeDtypeStruct((m, d), _F32), jax.ShapeDtypeStruct((m, d), _BF16),
                   jax.ShapeDtypeStruct((m, LANES), _F32), side_shape],
        scratch_shapes=[pltpu.VMEM((POOL_TM + 2 * POOL_HALO, d), _F32),
                        pltpu.VMEM((POOL_TM, d), _BF16),
                        pltpu.VMEM(w_pool.shape[1:], _BF16)],
        compiler_params=_params("arbitrary"),
        name="multiscale_pool",
    )(x, x, x, g.reshape(1, d), w_pool, scale.reshape(1, d), g_next.reshape(1, d), w_side)


def kernel(x, p, attn_norm_g, w_qkv, b_qkv, w_o, rpb, pool_norm_g, w_pool, pool_scale,
           ffn_norm_g, w_gate, w_up, w_down, ple_norm_g, w_ple_gate, b_ple_gate,
           w_ple_proj, final_norm_g):
    batch, seq, d = x.shape
    depth = p.shape[0]
    rows = seq // GRID_W
    assert d == D_MODEL and seq % GRID_W == 0 and rows >= WIN_H and seq % POOL_TM == 0
    m = batch * seq
    x = x.reshape(m, d)
    p = p.reshape(depth, m, p.shape[-1])
    q_scale = jnp.concatenate([jnp.full((d,), HEAD_DIM ** -0.5 * LOG2_E, _F32),
                               jnp.ones((2 * d,), _F32)])

    xg, ssq = norm_inputs(x, attn_norm_g[0])
    for i in range(depth):
        j = i // 2
        if i % 2 == 0:
            qkv = mm_qkv(xg, ssq, w_qkv, j, b_qkv[j], q_scale)
            o, w_down_bf16 = neighbourhood_attention(qkv, rpb[j], w_down, i, batch, seq)
            x, xg, ssq = mm_residual(o, w_o, j, x, ffn_norm_g[i], tm=1024)
        else:
            x, xg, ssq, w_down_bf16 = multiscale_pool(x, pool_norm_g[j], w_pool, j, pool_scale[j],
                                                      ffn_norm_g[i], w_down, i, seq)
        hidden = mm_swiglu(xg, ssq, w_gate, w_up, i)
        x, xg, ssq = mm_residual(hidden, w_down_bf16, None, x, ple_norm_g[i])
        ple = functools.partial(mm_ple, xg, ssq, w_ple_gate, b_ple_gate[i], p, w_ple_proj, i, x)
        if i + 1 == depth:
            x = ple(final_norm_g, "final")
        elif (i + 1) % 2 == 0:
            x, xg, ssq = ple(attn_norm_g[(i + 1) // 2], "emit")
        else:
            x = ple(None, "plain")
    return x.reshape(batch, seq, d)
```

```python
import functools

import jax
import jax.numpy as jnp
import numpy as np
from jax import lax
from jax.experimental import pallas as pl
from jax.experimental.pallas import tpu as pltpu

D_MODEL = 2048
GRID_W = 64
N_HEADS = 64
HEAD_DIM = D_MODEL // N_HEADS
WIN_H = 8
WIN_W = 16
POOL_WINDOWS = (2, 4, 8, 16)
GROUP_CH = D_MODEL // len(POOL_WINDOWS)
RMS_EPS = 1e-6

VMEM_LIMIT_BYTES = 56 * 1024 * 1024
LANES = 128

HEADS_PER_GROUP = 8
GROUP_LANES = HEADS_PER_GROUP * HEAD_DIM
KEY_COLS = 32
Q_GROUPS = ((0, 24, 0), (24, 16, 16), (40, 24, 32))
for _q0, _nq, _k0 in Q_GROUPS:
    _cs = np.clip(np.arange(_q0, _q0 + _nq) - WIN_W // 2, 0, GRID_W - WIN_W)
    assert _cs.min() >= _k0 and _cs.max() + WIN_W <= _k0 + KEY_COLS
MASK_BIAS = -1e30
LOG2_E = float(np.log2(np.e))

_BF16 = jnp.bfloat16
_F32 = jnp.float32


def _params(*semantics):
    return pltpu.CompilerParams(dimension_semantics=semantics,
                                vmem_limit_bytes=VMEM_LIMIT_BYTES)


def _dot(a, b):
    return jnp.dot(a, b, preferred_element_type=_F32)


def _row_scale(ssq_ref, d):
    total = ssq_ref[:, 0:LANES]
    for part in range(1, ssq_ref.shape[1] // LANES):
        total = total + ssq_ref[:, part * LANES:(part + 1) * LANES]
    return lax.rsqrt(total[:, 0:1] * (1.0 / d) + RMS_EPS)


def _emit_norm_inputs(x, g_ref, xg_ref, ssq_ref):
    xg_ref[...] = (x * g_ref[...]).astype(xg_ref.dtype)
    ssq_ref[...] = jnp.broadcast_to(jnp.sum(x * x, axis=-1, keepdims=True), ssq_ref.shape)


def _prep_kernel(x_ref, g_ref, xg_ref, ssq_ref):
    _emit_norm_inputs(x_ref[...], g_ref, xg_ref, ssq_ref)


def norm_inputs(x, g, tm=512):
    m, d = x.shape
    return pl.pallas_call(
        _prep_kernel,
        grid=(m // tm,),
        in_specs=[pl.BlockSpec((tm, d), lambda i: (i, 0)),
                  pl.BlockSpec((1, d), lambda i: (0, 0))],
        out_specs=[pl.BlockSpec((tm, d), lambda i: (i, 0)),
                   pl.BlockSpec((tm, LANES), lambda i: (i, 0))],
        out_shape=[jax.ShapeDtypeStruct((m, d), _BF16),
                   jax.ShapeDtypeStruct((m, LANES), _F32)],
        compiler_params=_params("parallel"),
        name="norm_inputs",
    )(x, g.reshape(1, d))


CAST_ROWS = 256


def _cast_weight(w_ref, wb_ref):
    @pl.when(pl.program_id(1) == 0)
    def _():
        def body(c, carry):
            rows = pl.ds(pl.multiple_of(c * CAST_ROWS, CAST_ROWS), CAST_ROWS)
            wb_ref[rows, :] = w_ref[rows, :].astype(wb_ref.dtype)
            return carry
        lax.fori_loop(0, w_ref.shape[0] // CAST_ROWS, body, 0)


def _col_halves(ref):
    half = ref.shape[1] // 2
    return (slice(0, half), slice(half, 2 * half))


ROW_CHUNK = 512


def _row_chunks(ref):
    n = ref.shape[0]
    step = min(n, ROW_CHUNK)
    return [slice(r0, r0 + step) for r0 in range(0, n, step)]


def _emit_norm_cols(x, rows, cols, g_ref, xg_ref):
    xg_ref[rows, cols] = (x * g_ref[:, cols]).astype(xg_ref.dtype)
    return jnp.sum(x * x, axis=-1, keepdims=True)


def _mm_qkv_kernel(d, xg_ref, ssq_ref, w_ref, b_ref, s_ref, o_ref, wb_ref):
    _cast_weight(w_ref, wb_ref)
    r_all = _row_scale(ssq_ref, d)
    for rows in _row_chunks(o_ref):
        a = xg_ref[rows, :]
        r = r_all[rows]
        for cols in _col_halves(o_ref):
            acc = _dot(a, wb_ref[:, cols]) * r
            o_ref[rows, cols] = ((acc + b_ref[:, cols]) * s_ref[:, cols]).astype(o_ref.dtype)


def _mm_swiglu_kernel(d, xg_ref, ssq_ref, wg_ref, wu_ref, o_ref, wgb_ref, wub_ref):
    _cast_weight(wg_ref, wgb_ref)
    _cast_weight(wu_ref, wub_ref)
    r_all = _row_scale(ssq_ref, d)
    for rows in _row_chunks(o_ref):
        a = xg_ref[rows, :]
        r = r_all[rows]
        for cols in _col_halves(o_ref):
            gate = _dot(a, wgb_ref[:, cols]) * r
            up = _dot(a, wub_ref[:, cols]) * r
            o_ref[rows, cols] = (gate * jax.nn.sigmoid(gate) * up).astype(o_ref.dtype)


def _mm_residual_kernel(a_ref, w_ref, res_ref, g_ref, x_ref, xg_ref, ssq_ref, *scratch):
    if scratch:
        wb_ref, = scratch
        _cast_weight(w_ref, wb_ref)
    else:
        wb_ref = w_ref
    for rows in _row_chunks(x_ref):
        a = a_ref[rows, :]
        ssq = 0.0
        for cols in _col_halves(x_ref):
            x = res_ref[rows, cols] + _dot(a, wb_ref[:, cols])
            x_ref[rows, cols] = x
            ssq = ssq + _emit_norm_cols(x, rows, cols, g_ref, xg_ref)
        ssq_ref[rows, :] = jnp.broadcast_to(ssq, (ssq.shape[0], ssq_ref.shape[1]))


def _mm_ple_kernel(d, mode, xg_ref, ssq_ref, wg_ref, b_ref, p_ref, wp_ref, res_ref, *rest):
    if mode == "emit":
        g_ref, x_ref, xg_out_ref, ssq_out_ref, wgb_ref, wpb_ref = rest
    elif mode == "final":
        g_ref, x_ref, wgb_ref, wpb_ref = rest
    else:
        x_ref, wgb_ref, wpb_ref = rest
    _cast_weight(wg_ref, wgb_ref)
    _cast_weight(wp_ref, wpb_ref)
    r_all = _row_scale(ssq_ref, d)
    for rows in _row_chunks(x_ref):
        a = xg_ref[rows, :]
        pb = p_ref[rows, :].astype(_BF16)
        r = r_all[rows]
        ssq = 0.0
        for cols in _col_halves(x_ref):
            gate = jax.nn.sigmoid(_dot(a, wgb_ref[:, cols]) * r + b_ref[:, cols])
            x = res_ref[rows, cols] + _dot(pb, wpb_ref[:, cols]) * gate
            x_ref[rows, cols] = x
            if mode == "emit":
                ssq = ssq + _emit_norm_cols(x, rows, cols, g_ref, xg_out_ref)
            elif mode == "final":
                ssq = ssq + jnp.sum(x * x, axis=-1, keepdims=True)
        if mode == "emit":
            ssq_out_ref[rows, :] = jnp.broadcast_to(ssq, (ssq.shape[0], ssq_out_ref.shape[1]))
        elif mode == "final":
            scale = lax.rsqrt(ssq * (1.0 / x_ref.shape[1]) + RMS_EPS)
            x_ref[rows, :] = x_ref[rows, :] * scale * g_ref[...]


def _row_spec(tm, k):
    return pl.BlockSpec((tm, k), lambda j, i: (i, 0))


def _col_spec(layer, k, tn, single_buffer=False):
    mode = pl.Buffered(1) if single_buffer else None
    return pl.BlockSpec((None, k, tn), lambda j, i: (layer, 0, j), pipeline_mode=mode)


def _vec_spec(tn):
    return pl.BlockSpec((1, tn), lambda j, i: (0, j))


def _tile_spec(tm, tn):
    return pl.BlockSpec((tm, tn), lambda j, i: (i, j))


def _ssq_out_spec(tm):
    return pl.BlockSpec((tm, LANES), lambda j, i: (i, j))


def _mm_params():
    return _params("parallel", "arbitrary")


def mm_qkv(xg, ssq, w, layer, b, s, tm=1024, tn=1024):
    m, k = xg.shape
    n = w.shape[-1]
    return pl.pallas_call(
        functools.partial(_mm_qkv_kernel, k),
        grid=(n // tn, m // tm),
        in_specs=[_row_spec(tm, k), _row_spec(tm, ssq.shape[1]), _col_spec(layer, k, tn),
                  _vec_spec(tn), _vec_spec(tn)],
        out_specs=_tile_spec(tm, tn),
        out_shape=jax.ShapeDtypeStruct((m, n), _BF16),
        scratch_shapes=[pltpu.VMEM((k, tn), _BF16)],
        compiler_params=_mm_params(),
        name="mm_qkv",
    )(xg, ssq, w, b.reshape(1, n), s.reshape(1, n))


def mm_swiglu(xg, ssq, wg, wu, layer, tm=2048, tn=512):
    m, k = xg.shape
    n = wg.shape[-1]
    return pl.pallas_call(
        functools.partial(_mm_swiglu_kernel, k),
        grid=(n // tn, m // tm),
        in_specs=[_row_spec(tm, k), _row_spec(tm, ssq.shape[1]),
                  _col_spec(layer, k, tn), _col_spec(layer, k, tn)],
        out_specs=_tile_spec(tm, tn),
        out_shape=jax.ShapeDtypeStruct((m, n), _BF16),
        scratch_shapes=[pltpu.VMEM((k, tn), _BF16), pltpu.VMEM((k, tn), _BF16)],
        compiler_params=_mm_params(),
        name="mm_swiglu",
    )(xg, ssq, wg, wu)


def _norm_out(m, n, tm, tn):
    specs = [_tile_spec(tm, tn), _tile_spec(tm, tn), _ssq_out_spec(tm)]
    shapes = [jax.ShapeDtypeStruct((m, n), _F32), jax.ShapeDtypeStruct((m, n), _BF16),
              jax.ShapeDtypeStruct((m, LANES * (n // tn)), _F32)]
    return specs, shapes


def mm_residual(a, w, layer, res, g_next, tm=512, tn=1024):
    m, k = a.shape
    n = w.shape[-1]
    out_specs, out_shape = _norm_out(m, n, tm, tn)
    if layer is None:
        w_spec, scratch = pl.BlockSpec((k, tn), lambda j, i: (0, j)), []
    else:
        w_spec, scratch = _col_spec(layer, k, tn), [pltpu.VMEM((k, tn), _BF16)]
    return pl.pallas_call(
        _mm_residual_kernel,
        grid=(n // tn, m // tm),
        in_specs=[_row_spec(tm, k), w_spec, _tile_spec(tm, tn), _vec_spec(tn)],
        out_specs=out_specs,
        out_shape=out_shape,
        scratch_shapes=scratch,
        compiler_params=_mm_params(),
        name="mm_residual",
    )(a, w, res, g_next.reshape(1, n))


def mm_ple(xg, ssq, wg, b, p, wp, layer, res, g, mode, tm=1024, tn=1024):
    m, k = xg.shape
    n = wg.shape[-1]
    kp = p.shape[-1]
    if mode == "final":
        tm, tn = 512, n
    in_specs = [_row_spec(tm, k), _row_spec(tm, ssq.shape[1]),
                _col_spec(layer, k, tn, single_buffer=True), _vec_spec(tn),
                pl.BlockSpec((None, tm, kp), lambda j, i: (layer, i, 0)),
                _col_spec(layer, kp, tn, single_buffer=True), _tile_spec(tm, tn)]
    args = [xg, ssq, wg, b.reshape(1, n), p, wp, res]
    if mode != "plain":
        in_specs.append(_vec_spec(tn))
        args.append(g.reshape(1, n))
    if mode == "emit":
        out_specs, out_shape = _norm_out(m, n, tm, tn)
    else:
        out_specs, out_shape = _tile_spec(tm, tn), jax.ShapeDtypeStruct((m, n), _F32)
    return pl.pallas_call(
        functools.partial(_mm_ple_kernel, k, mode),
        grid=(n // tn, m // tm),
        in_specs=in_specs,
        out_specs=out_specs,
        out_shape=out_shape,
        scratch_shapes=[pltpu.VMEM((k, tn), _BF16), pltpu.VMEM((kp, tn), _BF16)],
        compiler_params=_mm_params(),
        name="mm_ple",
    )(*args)


EXT_PAD = KEY_COLS // 2
N_BIAS_ROWS = 2 * WIN_H - 1
N_QUADS = N_BIAS_ROWS - LANES // KEY_COLS + 1
GROUP_ROWS = HEADS_PER_GROUP * GRID_W
PAIRS_PER_ITER = 4


def _build_bias_quads(ext_ref, quad_ref):
    lane = lax.broadcasted_iota(jnp.int32, (GRID_W, LANES), 1)
    qc = lax.broadcasted_iota(jnp.int32, (GRID_W, LANES), 0)
    lane_j = lane // KEY_COLS
    k0 = jnp.where(qc < Q_GROUPS[1][0], Q_GROUPS[0][2],
                   jnp.where(qc < Q_GROUPS[2][0], Q_GROUPS[1][2], Q_GROUPS[2][2]))
    kc = k0 + lane % KEY_COLS
    c_start = jnp.clip(qc - WIN_W // 2, 0, GRID_W - WIN_W)
    valid = (kc >= c_start) & (kc < c_start + WIN_W)

    def head_body(h, carry):
        def rolled(dr, j):
            row = ext_ref[0, pl.ds(h * N_BIAS_ROWS + dr, 1), :]
            parts = []
            for q0, nq, k0_g in Q_GROUPS:
                shift = (KEY_COLS * j - k0_g - (WIN_W - 1) - EXT_PAD + q0) % LANES
                parts.append(pltpu.roll(jnp.broadcast_to(row, (nq, LANES)), shift, 1,
                                        stride=1, stride_axis=0))
            return jnp.concatenate(parts, axis=0)

        for d0 in range(N_QUADS):
            quad = rolled(d0, 0)
            for j in range(1, LANES // KEY_COLS):
                quad = jnp.where(lane_j == j, rolled(d0 + j, j), quad)
            quad_ref[d0, pl.ds(pl.multiple_of(h * GRID_W, GRID_W), GRID_W), :] = (
                jnp.where(valid, quad * LOG2_E, MASK_BIAS))
        return carry

    lax.fori_loop(0, HEADS_PER_GROUP, head_body, 0)


def _attn_kernel(n_side, q_ref, k_ref, v_ref, ext_ref, *refs):
    side_in, (o_ref, *side_out) = refs[:n_side], refs[n_side:2 * n_side + 1]
    quad_ref, s0_ref, s1_ref, e0_ref, e1_ref, l0_ref, l1_ref = refs[2 * n_side + 1:]
    rows = q_ref.shape[0] // GRID_W
    for wd_ref, wdb_ref in zip(side_in, side_out):
        wdb_ref[...] = wd_ref[...].astype(wdb_ref.dtype)
    lane_head = lax.broadcasted_iota(jnp.int32, (1, GROUP_LANES), 1) // HEAD_DIM
    head_masks = [lane_head == h for h in range(HEADS_PER_GROUP)]

    @pl.when(pl.program_id(1) == 0)
    def _():
        _build_bias_quads(ext_ref, quad_ref)

    def key_rows(ref, r):
        r_start = jnp.clip(r - WIN_H // 2, 0, rows - WIN_H)
        return [ref[pl.ds(pl.multiple_of((r_start + j) * GRID_W, GRID_W), GRID_W), :]
                for j in range(WIN_H)]

    def window(slabs, k0):
        return jnp.concatenate([slab[k0:k0 + KEY_COLS] for slab in slabs], axis=0)

    def score_stage(r, s_ref):
        d0 = jnp.clip(r - WIN_H // 2, 0, rows - WIN_H) - r + (WIN_H - 1)
        q_row = q_ref[pl.ds(pl.multiple_of(r * GRID_W, GRID_W), GRID_W), :].astype(_F32)
        k_slabs = key_rows(k_ref, r)
        off = 0
        for q0, nq, k0 in Q_GROUPS:
            qg = q_row[q0:q0 + nq]
            qm = jnp.concatenate([jnp.where(mask, qg, 0.0) for mask in head_masks],
                                 axis=0).astype(_BF16)
            s = lax.dot_general(qm, window(k_slabs, k0), (((1,), (1,)), ((), ())),
                                preferred_element_type=_F32)
            bias = jnp.concatenate(
                [jnp.concatenate([quad_ref[d0 + half * (LANES // KEY_COLS),
                                           pl.ds(h * GRID_W + q0, nq), :]
                                  for h in range(HEADS_PER_GROUP)], axis=0)
                 for half in range(2)], axis=1)
            s_ref[pl.ds(off, HEADS_PER_GROUP * nq), :] = s + bias
            off += HEADS_PER_GROUP * nq

    def exp_stage(s_ref, e_ref, l_ref):
        off = 0
        for _, nq, _ in Q_GROUPS:
            rows_g = pl.ds(off, HEADS_PER_GROUP * nq)
            s = s_ref[rows_g, :]
            e = jnp.exp2(s - jnp.max(s, axis=-1, keepdims=True))
            e_ref[rows_g, :] = e.astype(e_ref.dtype)
            l_ref[rows_g, :] = jnp.broadcast_to(jnp.sum(e, axis=-1, keepdims=True),
                                                (HEADS_PER_GROUP * nq, LANES))
            off += HEADS_PER_GROUP * nq

    heads_per_vreg = LANES // HEAD_DIM
    half_masks = [head_masks[h][:, :LANES] for h in range(heads_per_vreg)]

    def value_stage(r, e_ref, l_ref):
        outs = []
        v_slabs = key_rows(v_ref, r)
        off = 0
        for q0, nq, k0 in Q_GROUPS:
            rows_g = pl.ds(off, HEADS_PER_GROUP * nq)
            o_all = _dot(e_ref[rows_g, :], window(v_slabs, k0))
            sums = l_ref[rows_g, :]
            o = jnp.where(head_masks[0], o_all[:nq], 0.0)
            for h in range(1, HEADS_PER_GROUP):
                o = o + jnp.where(head_masks[h], o_all[h * nq:(h + 1) * nq], 0.0)
            den = []
            for first in range(0, HEADS_PER_GROUP, heads_per_vreg):
                d = sums[first * nq:(first + 1) * nq]
                for h in range(1, heads_per_vreg):
                    d = jnp.where(half_masks[h], sums[(first + h) * nq:(first + h + 1) * nq], d)
                den.append(d)
            outs.append(o / jnp.concatenate(den, axis=1))
            off += HEADS_PER_GROUP * nq
        o_ref[pl.ds(pl.multiple_of(r * GRID_W, GRID_W), GRID_W), :] = (
            jnp.concatenate(outs, axis=0).astype(o_ref.dtype))

    score_stage(0, s0_ref)
    score_stage(1, s1_ref)
    exp_stage(s0_ref, e0_ref, l0_ref)

    def row_pair(r):
        score_stage(r, s0_ref)
        exp_stage(s1_ref, e1_ref, l1_ref)
        value_stage(r - 2, e0_ref, l0_ref)
        score_stage(r + 1, s1_ref)
        exp_stage(s0_ref, e0_ref, l0_ref)
        value_stage(r - 1, e1_ref, l1_ref)

    n_pairs = rows // 2 - 1
    n_iters = n_pairs // PAIRS_PER_ITER

    def loop_body(k, carry):
        for pair in range(PAIRS_PER_ITER):
            row_pair(2 * (PAIRS_PER_ITER * k + pair) + 2)
        return carry

    lax.fori_loop(0, n_iters, loop_body, 0)
    for pair in range(n_iters * PAIRS_PER_ITER, n_pairs):
        row_pair(2 * pair + 2)
    exp_stage(s1_ref, e1_ref, l1_ref)
    value_stage(rows - 2, e0_ref, l0_ref)
    value_stage(rows - 1, e1_ref, l1_ref)


def _attention_bias_ext(rpb):
    ext = jnp.pad(rpb, ((0, 0), (0, 0), (EXT_PAD, EXT_PAD)), mode="edge")
    ext = jnp.pad(ext, ((0, 0), (0, 0), (0, LANES - ext.shape[-1])))
    return ext.reshape(N_HEADS // HEADS_PER_GROUP, HEADS_PER_GROUP * N_BIAS_ROWS, LANES)


def _cast_share(w, steps):
    k, n = w.shape[1:]
    assert k % (steps * 16) == 0
    return (k // steps, n), jax.ShapeDtypeStruct((k, n), _BF16)


def neighbourhood_attention(qkv, rpb, w_side, side_layers, batch, seq):
    n_hg = N_HEADS // HEADS_PER_GROUP
    side_blk, side_shape = _cast_share(w_side, n_hg * batch)
    rows = seq // GRID_W
    assert rows % 2 == 0 and rows >= WIN_H
    blk = (seq, GROUP_LANES)
    ext = _attention_bias_ext(rpb)
    score_scratch = pltpu.VMEM((GROUP_ROWS, WIN_H * KEY_COLS), _F32)
    exp_scratch = pltpu.VMEM((GROUP_ROWS, WIN_H * KEY_COLS), _BF16)
    inv_scratch = pltpu.VMEM((GROUP_ROWS, LANES), _F32)
    o, *side = pl.pallas_call(
        functools.partial(_attn_kernel, len(side_layers)),
        grid=(n_hg, batch),
        in_specs=[pl.BlockSpec(blk, lambda hg, b: (b, hg)),
                  pl.BlockSpec(blk, lambda hg, b: (b, n_hg + hg)),
                  pl.BlockSpec(blk, lambda hg, b: (b, 2 * n_hg + hg)),
                  pl.BlockSpec((1,) + ext.shape[1:], lambda hg, b: (hg, 0, 0))]
                 + [pl.BlockSpec((None,) + side_blk,
                                 functools.partial(lambda layer, hg, b: (layer, hg * batch + b, 0), layer))
                    for layer in side_layers],
        out_specs=[pl.BlockSpec(blk, lambda hg, b: (b, hg))]
                  + [pl.BlockSpec(side_blk, lambda hg, b: (hg * batch + b, 0))] * len(side_layers),
        out_shape=[jax.ShapeDtypeStruct((batch * seq, D_MODEL), _BF16)]
                  + [side_shape] * len(side_layers),
        scratch_shapes=[pltpu.VMEM((N_QUADS, GROUP_ROWS, LANES), _F32),
                        score_scratch, score_scratch, exp_scratch, exp_scratch,
                        inv_scratch, inv_scratch],
        compiler_params=_params("arbitrary", "arbitrary"),
        name="nbr_attention",
    )(qkv, qkv, qkv, ext, *([w_side] * len(side_layers)))
    return o, side


POOL_TM = 512
POOL_HALO = 8


def _pool_kernel(tiles_per_seq, x_ref, prev_ref, next_ref, g_ref, w_ref, scale_ref, gn_ref,
                 o_ref, xg_ref, ssq_ref, hext_ref, mixed_ref, wb_ref):
    i = pl.program_id(0)
    n_ext = POOL_TM + 2 * POOL_HALO
    inner = slice(POOL_HALO, POOL_HALO + POOL_TM)

    def ahead(v, s):
        return pltpu.roll(v, n_ext - s, 0)

    @pl.when(i == 0)
    def _():
        wb_ref[...] = w_ref[...].astype(wb_ref.dtype)

    def rms(v):
        return v * lax.rsqrt(jnp.mean(v * v, axis=-1, keepdims=True) + RMS_EPS) * g_ref[...]

    ti = i % tiles_per_seq
    seq = tiles_per_seq * POOL_TM
    x = x_ref[...]
    h = rms(x)
    has_prev = (ti > 0).astype(_F32)
    has_next = (ti < tiles_per_seq - 1).astype(_F32)
    hext_ref[pl.ds(0, POOL_HALO), :] = rms(prev_ref[...]) * has_prev
    hext_ref[pl.ds(POOL_HALO, POOL_TM), :] = h
    hext_ref[pl.ds(POOL_HALO + POOL_TM, POOL_HALO), :] = rms(next_ref[...]) * has_next
    t = ti * POOL_TM + lax.broadcasted_iota(jnp.int32, (POOL_TM, 1), 0)
    outs = []
    for gi, w in enumerate(POOL_WINDOWS):
        half = w // 2
        assert half <= POOL_HALO
        lo = jnp.clip(t - half, 0, seq)
        hi = jnp.clip(t + w - half, 0, seq)
        inv_cnt = 1.0 / (hi - lo).astype(_F32)
        for c in range(GROUP_CH // LANES):
            cols = slice(gi * GROUP_CH + c * LANES, gi * GROUP_CH + (c + 1) * LANES)
            hx = hext_ref[:, cols]
            run, span = hx, 1
            while span < half:
                run = run + ahead(run, span)
                span *= 2
            first = run if half == POOL_HALO else ahead(run, POOL_HALO - half)
            win = first[:POOL_TM] + run[inner]
            mixed_ref[:, cols] = (win * inv_cnt - hx[inner]).astype(mixed_ref.dtype)
        cols = slice(gi * GROUP_CH, (gi + 1) * GROUP_CH)
        outs.append(x[:, cols] + _dot(mixed_ref[:, cols], wb_ref[gi]) * scale_ref[:, cols])
    x_new = jnp.concatenate(outs, axis=1)
    o_ref[...] = x_new
    _emit_norm_inputs(x_new, gn_ref, xg_ref, ssq_ref)


def multiscale_pool(x, g, w_pool, layer, scale, g_next, seq):
    m, d = x.shape
    tiles_per_seq = seq // POOL_TM
    halo_blocks = POOL_TM // POOL_HALO
    last_halo = m // POOL_HALO - 1
    vec = pl.BlockSpec((1, d), lambda i: (0, 0))
    return pl.pallas_call(
        functools.partial(_pool_kernel, tiles_per_seq),
        grid=(m // POOL_TM,),
        in_specs=[pl.BlockSpec((POOL_TM, d), lambda i: (i, 0)),
                  pl.BlockSpec((POOL_HALO, d), lambda i: (jnp.maximum(i * halo_blocks - 1, 0), 0)),
                  pl.BlockSpec((POOL_HALO, d),
                               lambda i: (jnp.minimum((i + 1) * halo_blocks, last_halo), 0)),
                  vec,
                  pl.BlockSpec((None,) + w_pool.shape[1:], lambda i: (layer, 0, 0, 0)),
                  vec, vec],
        out_specs=[pl.BlockSpec((POOL_TM, d), lambda i: (i, 0)),
                   pl.BlockSpec((POOL_TM, d), lambda i: (i, 0)),
                   pl.BlockSpec((POOL_TM, LANES), lambda i: (i, 0))],
        out_shape=[jax.ShapeDtypeStruct((m, d), _F32), jax.ShapeDtypeStruct((m, d), _BF16),
                   jax.ShapeDtypeStruct((m, LANES), _F32)],
        scratch_shapes=[pltpu.VMEM((POOL_TM + 2 * POOL_HALO, d), _F32),
                        pltpu.VMEM((POOL_TM, d), _BF16),
                        pltpu.VMEM(w_pool.shape[1:], _BF16)],
        compiler_params=_params("arbitrary"),
        name="multiscale_pool",
    )(x, x, x, g.reshape(1, d), w_pool, scale.reshape(1, d), g_next.reshape(1, d))


def kernel(x, p, attn_norm_g, w_qkv, b_qkv, w_o, rpb, pool_norm_g, w_pool, pool_scale,
           ffn_norm_g, w_gate, w_up, w_down, ple_norm_g, w_ple_gate, b_ple_gate,
           w_ple_proj, final_norm_g):
    batch, seq, d = x.shape
    depth = p.shape[0]
    rows = seq // GRID_W
    assert d == D_MODEL and seq % GRID_W == 0 and rows >= WIN_H and seq % POOL_TM == 0
    m = batch * seq
    x = x.reshape(m, d)
    p = p.reshape(depth, m, p.shape[-1])
    q_scale = jnp.concatenate([jnp.full((d,), HEAD_DIM ** -0.5 * LOG2_E, _F32),
                               jnp.ones((2 * d,), _F32)])

    xg, ssq = norm_inputs(x, attn_norm_g[0])
    for i in range(depth):
        j = i // 2
        if i % 2 == 0:
            qkv = mm_qkv(xg, ssq, w_qkv, j, b_qkv[j], q_scale)
            side_layers = list(range(i, min(i + 2, depth)))
            o, side = neighbourhood_attention(qkv, rpb[j], w_down, side_layers, batch, seq)
            w_down_bf16 = dict(zip(side_layers, side))
            x, xg, ssq = mm_residual(o, w_o, j, x, ffn_norm_g[i], tm=1024)
        else:
            x, xg, ssq = multiscale_pool(x, pool_norm_g[j], w_pool, j, pool_scale[j],
                                         ffn_norm_g[i], seq)
        hidden = mm_swiglu(xg, ssq, w_gate, w_up, i)
        x, xg, ssq = mm_residual(hidden, w_down_bf16[i], None, x, ple_norm_g[i])
        ple = functools.partial(mm_ple, xg, ssq, w_ple_gate, b_ple_gate[i], p, w_ple_proj, i, x)
        if i + 1 == depth:
            x = ple(final_norm_g, "final")
        elif (i + 1) % 2 == 0:
            x, xg, ssq = ple(attn_norm_g[(i + 1) // 2], "emit")
        else:
            x = ple(None, "plain")
    return x.reshape(batch, seq, d)
```

```python
import functools

import jax
import jax.numpy as jnp
import numpy as np
from jax import lax
from jax.experimental import pallas as pl
from jax.experimental.pallas import tpu as pltpu

D_MODEL = 2048
GRID_W = 64
N_HEADS = 64
HEAD_DIM = D_MODEL // N_HEADS
WIN_H = 8
WIN_W = 16
POOL_WINDOWS = (2, 4, 8, 16)
GROUP_CH = D_MODEL // len(POOL_WINDOWS)
RMS_EPS = 1e-6

VMEM_LIMIT_BYTES = 56 * 1024 * 1024
LANES = 128

HEADS_PER_GROUP = 8
GROUP_LANES = HEADS_PER_GROUP * HEAD_DIM
KEY_COLS = 32
Q_GROUPS = ((0, 24, 0), (24, 16, 16), (40, 24, 32))
for _q0, _nq, _k0 in Q_GROUPS:
    _cs = np.clip(np.arange(_q0, _q0 + _nq) - WIN_W // 2, 0, GRID_W - WIN_W)
    assert _cs.min() >= _k0 and _cs.max() + WIN_W <= _k0 + KEY_COLS
MASK_BIAS = -1e30
LOG2_E = float(np.log2(np.e))

_BF16 = jnp.bfloat16
_F32 = jnp.float32


def _params(*semantics):
    return pltpu.CompilerParams(dimension_semantics=semantics,
                                vmem_limit_bytes=VMEM_LIMIT_BYTES)


def _dot(a, b):
    return jnp.dot(a, b, preferred_element_type=_F32)


def _row_scale(ssq_ref, d):
    total = ssq_ref[:, 0:LANES]
    for part in range(1, ssq_ref.shape[1] // LANES):
        total = total + ssq_ref[:, part * LANES:(part + 1) * LANES]
    return lax.rsqrt(total[:, 0:1] * (1.0 / d) + RMS_EPS)


def _emit_norm_inputs(x, g_ref, xg_ref, ssq_ref):
    xg_ref[...] = (x * g_ref[...]).astype(xg_ref.dtype)
    ssq_ref[...] = jnp.broadcast_to(jnp.sum(x * x, axis=-1, keepdims=True), ssq_ref.shape)


CAST_ROWS = 256


def _cast_weight(w_ref, wb_ref):
    @pl.when(pl.program_id(1) == 0)
    def _():
        def body(c, carry):
            rows = pl.ds(pl.multiple_of(c * CAST_ROWS, CAST_ROWS), CAST_ROWS)
            wb_ref[rows, :] = w_ref[rows, :].astype(wb_ref.dtype)
            return carry
        lax.fori_loop(0, w_ref.shape[0] // CAST_ROWS, body, 0)


def _col_halves(ref):
    half = ref.shape[1] // 2
    return (slice(0, half), slice(half, 2 * half))


ROW_CHUNK = 256


def _row_chunks(ref):
    n = ref.shape[0]
    step = min(n, ROW_CHUNK)
    return [slice(r0, r0 + step) for r0 in range(0, n, step)]


def _emit_norm_cols(x, rows, cols, g_ref, xg_ref):
    xg_ref[rows, cols] = (x * g_ref[:, cols]).astype(xg_ref.dtype)
    return jnp.sum(x * x, axis=-1, keepdims=True)


def _mm_qkv_kernel(d, from_x, a_ref, aux_ref, w_ref, b_ref, s_ref, o_ref, wb_ref):
    _cast_weight(w_ref, wb_ref)
    r_all = None if from_x else _row_scale(aux_ref, d)
    for rows in _row_chunks(o_ref):
        if from_x:
            x = a_ref[rows, :]
            a = (x * aux_ref[...]).astype(_BF16)
            r = lax.rsqrt(jnp.mean(x * x, axis=-1, keepdims=True) + RMS_EPS)
        else:
            a = a_ref[rows, :]
            r = r_all[rows]
        for cols in _col_halves(o_ref):
            acc = _dot(a, wb_ref[:, cols]) * r
            o_ref[rows, cols] = ((acc + b_ref[:, cols]) * s_ref[:, cols]).astype(o_ref.dtype)


def _mm_swiglu_kernel(d, xg_ref, ssq_ref, wg_ref, wu_ref, o_ref, wgb_ref, wub_ref):
    _cast_weight(wg_ref, wgb_ref)
    _cast_weight(wu_ref, wub_ref)
    r_all = _row_scale(ssq_ref, d)
    for rows in _row_chunks(o_ref):
        a = xg_ref[rows, :]
        r = r_all[rows]
        for cols in _col_halves(o_ref):
            gate = _dot(a, wgb_ref[:, cols]) * r
            up = _dot(a, wub_ref[:, cols]) * r
            o_ref[rows, cols] = (gate * jax.nn.sigmoid(gate) * up).astype(o_ref.dtype)


def _mm_residual_kernel(a_ref, w_ref, res_ref, g_ref, x_ref, xg_ref, ssq_ref, *scratch):
    if scratch:
        wb_ref, = scratch
        _cast_weight(w_ref, wb_ref)
    else:
        wb_ref = w_ref
    for rows in _row_chunks(x_ref):
        a = a_ref[rows, :]
        ssq = 0.0
        for cols in _col_halves(x_ref):
            x = res_ref[rows, cols] + _dot(a, wb_ref[:, cols])
            x_ref[rows, cols] = x
            ssq = ssq + _emit_norm_cols(x, rows, cols, g_ref, xg_ref)
        ssq_ref[rows, :] = jnp.broadcast_to(ssq, (ssq.shape[0], ssq_ref.shape[1]))


def _mm_ple_kernel(d, mode, xg_ref, ssq_ref, wg_ref, b_ref, p_ref, wp_ref, res_ref, *rest):
    if mode == "emit":
        g_ref, x_ref, xg_out_ref, ssq_out_ref, wgb_ref, wpb_ref = rest
    elif mode == "final":
        g_ref, x_ref, wgb_ref, wpb_ref = rest
    else:
        x_ref, wgb_ref, wpb_ref = rest
    _cast_weight(wg_ref, wgb_ref)
    _cast_weight(wp_ref, wpb_ref)
    r_all = _row_scale(ssq_ref, d)
    for rows in _row_chunks(x_ref):
        a = xg_ref[rows, :]
        pb = p_ref[rows, :].astype(_BF16)
        r = r_all[rows]
        ssq = 0.0
        for cols in _col_halves(x_ref):
            gate = jax.nn.sigmoid(_dot(a, wgb_ref[:, cols]) * r + b_ref[:, cols])
            x = res_ref[rows, cols] + _dot(pb, wpb_ref[:, cols]) * gate
            x_ref[rows, cols] = x
            if mode == "emit":
                ssq = ssq + _emit_norm_cols(x, rows, cols, g_ref, xg_out_ref)
            elif mode == "final":
                ssq = ssq + jnp.sum(x * x, axis=-1, keepdims=True)
        if mode == "emit":
            ssq_out_ref[rows, :] = jnp.broadcast_to(ssq, (ssq.shape[0], ssq_out_ref.shape[1]))
        elif mode == "final":
            scale = lax.rsqrt(ssq * (1.0 / x_ref.shape[1]) + RMS_EPS)
            x_ref[rows, :] = x_ref[rows, :] * scale * g_ref[...]


def _row_spec(tm, k):
    return pl.BlockSpec((tm, k), lambda j, i: (i, 0))


def _col_spec(layer, k, tn, single_buffer=False):
    mode = pl.Buffered(1) if single_buffer else None
    return pl.BlockSpec((None, k, tn), lambda j, i: (layer, 0, j), pipeline_mode=mode)


def _vec_spec(tn):
    return pl.BlockSpec((1, tn), lambda j, i: (0, j))


def _tile_spec(tm, tn):
    return pl.BlockSpec((tm, tn), lambda j, i: (i, j))


def _ssq_out_spec(tm):
    return pl.BlockSpec((tm, LANES), lambda j, i: (i, j))


def _mm_params():
    return _params("parallel", "arbitrary")


def mm_qkv(a, aux, w, layer, b, s, tm=1024, tn=1024):
    m, k = a.shape
    n = w.shape[-1]
    from_x = a.dtype == _F32
    aux_spec = pl.BlockSpec((1, k), lambda j, i: (0, 0)) if from_x else _row_spec(tm, aux.shape[1])
    aux = aux.reshape(1, k) if from_x else aux
    return pl.pallas_call(
        functools.partial(_mm_qkv_kernel, k, from_x),
        grid=(n // tn, m // tm),
        in_specs=[_row_spec(tm, k), aux_spec, _col_spec(layer, k, tn),
                  _vec_spec(tn), _vec_spec(tn)],
        out_specs=_tile_spec(tm, tn),
        out_shape=jax.ShapeDtypeStruct((m, n), _BF16),
        scratch_shapes=[pltpu.VMEM((k, tn), _BF16)],
        compiler_params=_mm_params(),
        name="mm_qkv",
    )(a, aux, w, b.reshape(1, n), s.reshape(1, n))


def mm_swiglu(xg, ssq, wg, wu, layer, tm=2048, tn=512):
    m, k = xg.shape
    n = wg.shape[-1]
    return pl.pallas_call(
        functools.partial(_mm_swiglu_kernel, k),
        grid=(n // tn, m // tm),
        in_specs=[_row_spec(tm, k), _row_spec(tm, ssq.shape[1]),
                  _col_spec(layer, k, tn), _col_spec(layer, k, tn)],
        out_specs=_tile_spec(tm, tn),
        out_shape=jax.ShapeDtypeStruct((m, n), _BF16),
        scratch_shapes=[pltpu.VMEM((k, tn), _BF16), pltpu.VMEM((k, tn), _BF16)],
        compiler_params=_mm_params(),
        name="mm_swiglu",
    )(xg, ssq, wg, wu)


def _norm_out(m, n, tm, tn):
    specs = [_tile_spec(tm, tn), _tile_spec(tm, tn), _ssq_out_spec(tm)]
    shapes = [jax.ShapeDtypeStruct((m, n), _F32), jax.ShapeDtypeStruct((m, n), _BF16),
              jax.ShapeDtypeStruct((m, LANES * (n // tn)), _F32)]
    return specs, shapes


def mm_residual(a, w, layer, res, g_next, tm=512, tn=1024):
    m, k = a.shape
    n = w.shape[-1]
    out_specs, out_shape = _norm_out(m, n, tm, tn)
    if layer is None:
        w_spec, scratch = pl.BlockSpec((k, tn), lambda j, i: (0, j)), []
    else:
        w_spec, scratch = _col_spec(layer, k, tn), [pltpu.VMEM((k, tn), _BF16)]
    return pl.pallas_call(
        _mm_residual_kernel,
        grid=(n // tn, m // tm),
        in_specs=[_row_spec(tm, k), w_spec, _tile_spec(tm, tn), _vec_spec(tn)],
        out_specs=out_specs,
        out_shape=out_shape,
        scratch_shapes=scratch,
        compiler_params=_mm_params(),
        name="mm_residual",
    )(a, w, res, g_next.reshape(1, n))


def mm_ple(xg, ssq, wg, b, p, wp, layer, res, g, mode, tm=1024, tn=1024):
    m, k = xg.shape
    n = wg.shape[-1]
    kp = p.shape[-1]
    if mode == "final":
        tm, tn = 512, n
    in_specs = [_row_spec(tm, k), _row_spec(tm, ssq.shape[1]),
                _col_spec(layer, k, tn, single_buffer=True), _vec_spec(tn),
                pl.BlockSpec((None, tm, kp), lambda j, i: (layer, i, 0)),
                _col_spec(layer, kp, tn, single_buffer=True), _tile_spec(tm, tn)]
    args = [xg, ssq, wg, b.reshape(1, n), p, wp, res]
    if mode != "plain":
        in_specs.append(_vec_spec(tn))
        args.append(g.reshape(1, n))
    if mode == "emit":
        out_specs, out_shape = _norm_out(m, n, tm, tn)
    else:
        out_specs, out_shape = _tile_spec(tm, tn), jax.ShapeDtypeStruct((m, n), _F32)
    return pl.pallas_call(
        functools.partial(_mm_ple_kernel, k, mode),
        grid=(n // tn, m // tm),
        in_specs=in_specs,
        out_specs=out_specs,
        out_shape=out_shape,
        scratch_shapes=[pltpu.VMEM((k, tn), _BF16), pltpu.VMEM((kp, tn), _BF16)],
        compiler_params=_mm_params(),
        name="mm_ple",
    )(*args)


EXT_PAD = KEY_COLS // 2
N_BIAS_ROWS = 2 * WIN_H - 1
N_QUADS = N_BIAS_ROWS - LANES // KEY_COLS + 1
GROUP_ROWS = HEADS_PER_GROUP * GRID_W
PAIRS_PER_ITER = 4


def _build_bias_quads(ext_ref, quad_ref):
    lane = lax.broadcasted_iota(jnp.int32, (GRID_W, LANES), 1)
    qc = lax.broadcasted_iota(jnp.int32, (GRID_W, LANES), 0)
    lane_j = lane // KEY_COLS
    k0 = jnp.where(qc < Q_GROUPS[1][0], Q_GROUPS[0][2],
                   jnp.where(qc < Q_GROUPS[2][0], Q_GROUPS[1][2], Q_GROUPS[2][2]))
    kc = k0 + lane % KEY_COLS
    c_start = jnp.clip(qc - WIN_W // 2, 0, GRID_W - WIN_W)
    valid = (kc >= c_start) & (kc < c_start + WIN_W)

    def head_body(h, carry):
        def rolled(dr, j):
            row = ext_ref[0, pl.ds(h * N_BIAS_ROWS + dr, 1), :]
            parts = []
            for q0, nq, k0_g in Q_GROUPS:
                shift = (KEY_COLS * j - k0_g - (WIN_W - 1) - EXT_PAD + q0) % LANES
                parts.append(pltpu.roll(jnp.broadcast_to(row, (nq, LANES)), shift, 1,
                                        stride=1, stride_axis=0))
            return jnp.concatenate(parts, axis=0)

        for d0 in range(N_QUADS):
            quad = rolled(d0, 0)
            for j in range(1, LANES // KEY_COLS):
                quad = jnp.where(lane_j == j, rolled(d0 + j, j), quad)
            quad_ref[d0, pl.ds(pl.multiple_of(h * GRID_W, GRID_W), GRID_W), :] = (
                jnp.where(valid, quad * LOG2_E, MASK_BIAS))
        return carry

    lax.fori_loop(0, HEADS_PER_GROUP, head_body, 0)


def _attn_kernel(n_side, q_ref, k_ref, v_ref, ext_ref, *refs):
    side_in, (o_ref, *side_out) = refs[:n_side], refs[n_side:2 * n_side + 1]
    quad_ref, s0_ref, s1_ref, e0_ref, e1_ref, l0_ref, l1_ref = refs[2 * n_side + 1:]
    rows = q_ref.shape[0] // GRID_W
    for wd_ref, wdb_ref in zip(side_in, side_out):
        wdb_ref[...] = wd_ref[...].astype(wdb_ref.dtype)
    lane_head = lax.broadcasted_iota(jnp.int32, (1, GROUP_LANES), 1) // HEAD_DIM
    head_masks = [lane_head == h for h in range(HEADS_PER_GROUP)]

    @pl.when(pl.program_id(1) == 0)
    def _():
        _build_bias_quads(ext_ref, quad_ref)

    def key_rows(ref, r):
        r_start = jnp.clip(r - WIN_H // 2, 0, rows - WIN_H)
        return [ref[pl.ds(pl.multiple_of((r_start + j) * GRID_W, GRID_W), GRID_W), :]
                for j in range(WIN_H)]

    def window(slabs, k0):
        return jnp.concatenate([slab[k0:k0 + KEY_COLS] for slab in slabs], axis=0)

    def score_stage(r, s_ref):
        d0 = jnp.clip(r - WIN_H // 2, 0, rows - WIN_H) - r + (WIN_H - 1)
        q_row = q_ref[pl.ds(pl.multiple_of(r * GRID_W, GRID_W), GRID_W), :].astype(_F32)
        k_slabs = key_rows(k_ref, r)
        off = 0
        for q0, nq, k0 in Q_GROUPS:
            qg = q_row[q0:q0 + nq]
            qm = jnp.concatenate([jnp.where(mask, qg, 0.0) for mask in head_masks],
                                 axis=0).astype(_BF16)
            s = lax.dot_general(qm, window(k_slabs, k0), (((1,), (1,)), ((), ())),
                                preferred_element_type=_F32)
            bias = jnp.concatenate(
                [jnp.concatenate([quad_ref[d0 + half * (LANES // KEY_COLS),
                                           pl.ds(h * GRID_W + q0, nq), :]
                                  for h in range(HEADS_PER_GROUP)], axis=0)
                 for half in range(2)], axis=1)
            s_ref[pl.ds(off, HEADS_PER_GROUP * nq), :] = s + bias
            off += HEADS_PER_GROUP * nq

    def exp_stage(s_ref, e_ref, l_ref):
        off = 0
        for _, nq, _ in Q_GROUPS:
            rows_g = pl.ds(off, HEADS_PER_GROUP * nq)
            s = s_ref[rows_g, :]
            e = jnp.exp2(s - jnp.max(s, axis=-1, keepdims=True))
            e_ref[rows_g, :] = e.astype(e_ref.dtype)
            l_ref[rows_g, :] = jnp.broadcast_to(jnp.sum(e, axis=-1, keepdims=True),
                                                (HEADS_PER_GROUP * nq, LANES))
            off += HEADS_PER_GROUP * nq

    heads_per_vreg = LANES // HEAD_DIM
    half_masks = [head_masks[h][:, :LANES] for h in range(heads_per_vreg)]

    def value_stage(r, e_ref, l_ref):
        outs = []
        v_slabs = key_rows(v_ref, r)
        off = 0
        for q0, nq, k0 in Q_GROUPS:
            rows_g = pl.ds(off, HEADS_PER_GROUP * nq)
            o_all = _dot(e_ref[rows_g, :], window(v_slabs, k0))
            sums = l_ref[rows_g, :]
            o = jnp.where(head_masks[0], o_all[:nq], 0.0)
            for h in range(1, HEADS_PER_GROUP):
                o = o + jnp.where(head_masks[h], o_all[h * nq:(h + 1) * nq], 0.0)
            den = []
            for first in range(0, HEADS_PER_GROUP, heads_per_vreg):
                d = sums[first * nq:(first + 1) * nq]
                for h in range(1, heads_per_vreg):
                    d = jnp.where(half_masks[h], sums[(first + h) * nq:(first + h + 1) * nq], d)
                den.append(d)
            outs.append(o / jnp.concatenate(den, axis=1))
            off += HEADS_PER_GROUP * nq
        o_ref[pl.ds(pl.multiple_of(r * GRID_W, GRID_W), GRID_W), :] = (
            jnp.concatenate(outs, axis=0).astype(o_ref.dtype))

    score_stage(0, s0_ref)
    score_stage(1, s1_ref)
    exp_stage(s0_ref, e0_ref, l0_ref)

    def row_pair(r):
        score_stage(r, s0_ref)
        exp_stage(s1_ref, e1_ref, l1_ref)
        value_stage(r - 2, e0_ref, l0_ref)
        score_stage(r + 1, s1_ref)
        exp_stage(s0_ref, e0_ref, l0_ref)
        value_stage(r - 1, e1_ref, l1_ref)

    n_pairs = rows // 2 - 1
    n_iters = n_pairs // PAIRS_PER_ITER

    def loop_body(k, carry):
        for pair in range(PAIRS_PER_ITER):
            row_pair(2 * (PAIRS_PER_ITER * k + pair) + 2)
        return carry

    lax.fori_loop(0, n_iters, loop_body, 0)
    for pair in range(n_iters * PAIRS_PER_ITER, n_pairs):
        row_pair(2 * pair + 2)
    exp_stage(s1_ref, e1_ref, l1_ref)
    value_stage(rows - 2, e0_ref, l0_ref)
    value_stage(rows - 1, e1_ref, l1_ref)


def _attention_bias_ext(rpb):
    ext = jnp.pad(rpb, ((0, 0), (0, 0), (EXT_PAD, EXT_PAD)), mode="edge")
    ext = jnp.pad(ext, ((0, 0), (0, 0), (0, LANES - ext.shape[-1])))
    return ext.reshape(N_HEADS // HEADS_PER_GROUP, HEADS_PER_GROUP * N_BIAS_ROWS, LANES)


def _cast_share(w, steps):
    k, n = w.shape[1:]
    assert k % (steps * 16) == 0
    return (k // steps, n), jax.ShapeDtypeStruct((k, n), _BF16)


def neighbourhood_attention(qkv, rpb, w_side, side_layers, batch, seq):
    n_hg = N_HEADS // HEADS_PER_GROUP
    side_blk, side_shape = _cast_share(w_side, n_hg * batch)
    rows = seq // GRID_W
    assert rows % 2 == 0 and rows >= WIN_H
    blk = (seq, GROUP_LANES)
    ext = _attention_bias_ext(rpb)
    score_scratch = pltpu.VMEM((GROUP_ROWS, WIN_H * KEY_COLS), _F32)
    exp_scratch = pltpu.VMEM((GROUP_ROWS, WIN_H * KEY_COLS), _BF16)
    inv_scratch = pltpu.VMEM((GROUP_ROWS, LANES), _F32)
    o, *side = pl.pallas_call(
        functools.partial(_attn_kernel, len(side_layers)),
        grid=(n_hg, batch),
        in_specs=[pl.BlockSpec(blk, lambda hg, b: (b, hg)),
                  pl.BlockSpec(blk, lambda hg, b: (b, n_hg + hg)),
                  pl.BlockSpec(blk, lambda hg, b: (b, 2 * n_hg + hg)),
                  pl.BlockSpec((1,) + ext.shape[1:], lambda hg, b: (hg, 0, 0))]
                 + [pl.BlockSpec((None,) + side_blk,
                                 functools.partial(lambda layer, hg, b: (layer, hg * batch + b, 0), layer))
                    for layer in side_layers],
        out_specs=[pl.BlockSpec(blk, lambda hg, b: (b, hg))]
                  + [pl.BlockSpec(side_blk, lambda hg, b: (hg * batch + b, 0))] * len(side_layers),
        out_shape=[jax.ShapeDtypeStruct((batch * seq, D_MODEL), _BF16)]
                  + [side_shape] * len(side_layers),
        scratch_shapes=[pltpu.VMEM((N_QUADS, GROUP_ROWS, LANES), _F32),
                        score_scratch, score_scratch, exp_scratch, exp_scratch,
                        inv_scratch, inv_scratch],
        compiler_params=_params("arbitrary", "arbitrary"),
        name="nbr_attention",
    )(qkv, qkv, qkv, ext, *([w_side] * len(side_layers)))
    return o, side


POOL_TM = 512
POOL_HALO = 8


def _pool_kernel(tiles_per_seq, x_ref, prev_ref, next_ref, g_ref, w_ref, scale_ref, gn_ref,
                 o_ref, xg_ref, ssq_ref, hext_ref, mixed_ref, wb_ref):
    i = pl.program_id(0)
    n_ext = POOL_TM + 2 * POOL_HALO
    inner = slice(POOL_HALO, POOL_HALO + POOL_TM)

    def ahead(v, s):
        return pltpu.roll(v, n_ext - s, 0)

    @pl.when(i == 0)
    def _():
        wb_ref[...] = w_ref[...].astype(wb_ref.dtype)

    def rms(v):
        return v * lax.rsqrt(jnp.mean(v * v, axis=-1, keepdims=True) + RMS_EPS) * g_ref[...]

    ti = i % tiles_per_seq
    seq = tiles_per_seq * POOL_TM
    x = x_ref[...]
    h = rms(x)
    has_prev = (ti > 0).astype(_F32)
    has_next = (ti < tiles_per_seq - 1).astype(_F32)
    hext_ref[pl.ds(0, POOL_HALO), :] = rms(prev_ref[...]) * has_prev
    hext_ref[pl.ds(POOL_HALO, POOL_TM), :] = h
    hext_ref[pl.ds(POOL_HALO + POOL_TM, POOL_HALO), :] = rms(next_ref[...]) * has_next
    t = ti * POOL_TM + lax.broadcasted_iota(jnp.int32, (POOL_TM, 1), 0)
    outs = []
    for gi, w in enumerate(POOL_WINDOWS):
        half = w // 2
        assert half <= POOL_HALO
        lo = jnp.clip(t - half, 0, seq)
        hi = jnp.clip(t + w - half, 0, seq)
        inv_cnt = 1.0 / (hi - lo).astype(_F32)
        for c in range(GROUP_CH // LANES):
            cols = slice(gi * GROUP_CH + c * LANES, gi * GROUP_CH + (c + 1) * LANES)
            hx = hext_ref[:, cols]
            run, span = hx, 1
            while span < half:
                run = run + ahead(run, span)
                span *= 2
            first = run if half == POOL_HALO else ahead(run, POOL_HALO - half)
            win = first[:POOL_TM] + run[inner]
            mixed_ref[:, cols] = (win * inv_cnt - hx[inner]).astype(mixed_ref.dtype)
        cols = slice(gi * GROUP_CH, (gi + 1) * GROUP_CH)
        outs.append(x[:, cols] + _dot(mixed_ref[:, cols], wb_ref[gi]) * scale_ref[:, cols])
    x_new = jnp.concatenate(outs, axis=1)
    o_ref[...] = x_new
    _emit_norm_inputs(x_new, gn_ref, xg_ref, ssq_ref)


def multiscale_pool(x, g, w_pool, layer, scale, g_next, seq):
    m, d = x.shape
    tiles_per_seq = seq // POOL_TM
    halo_blocks = POOL_TM // POOL_HALO
    last_halo = m // POOL_HALO - 1
    vec = pl.BlockSpec((1, d), lambda i: (0, 0))
    return pl.pallas_call(
        functools.partial(_pool_kernel, tiles_per_seq),
        grid=(m // POOL_TM,),
        in_specs=[pl.BlockSpec((POOL_TM, d), lambda i: (i, 0)),
                  pl.BlockSpec((POOL_HALO, d), lambda i: (jnp.maximum(i * halo_blocks - 1, 0), 0)),
                  pl.BlockSpec((POOL_HALO, d),
                               lambda i: (jnp.minimum((i + 1) * halo_blocks, last_halo), 0)),
                  vec,
                  pl.BlockSpec((None,) + w_pool.shape[1:], lambda i: (layer, 0, 0, 0)),
                  vec, vec],
        out_specs=[pl.BlockSpec((POOL_TM, d), lambda i: (i, 0)),
                   pl.BlockSpec((POOL_TM, d), lambda i: (i, 0)),
                   pl.BlockSpec((POOL_TM, LANES), lambda i: (i, 0))],
        out_shape=[jax.ShapeDtypeStruct((m, d), _F32), jax.ShapeDtypeStruct((m, d), _BF16),
                   jax.ShapeDtypeStruct((m, LANES), _F32)],
        scratch_shapes=[pltpu.VMEM((POOL_TM + 2 * POOL_HALO, d), _F32),
                        pltpu.VMEM((POOL_TM, d), _BF16),
                        pltpu.VMEM(w_pool.shape[1:], _BF16)],
        compiler_params=_params("arbitrary"),
        name="multiscale_pool",
    )(x, x, x, g.reshape(1, d), w_pool, scale.reshape(1, d), g_next.reshape(1, d))


def kernel(x, p, attn_norm_g, w_qkv, b_qkv, w_o, rpb, pool_norm_g, w_pool, pool_scale,
           ffn_norm_g, w_gate, w_up, w_down, ple_norm_g, w_ple_gate, b_ple_gate,
           w_ple_proj, final_norm_g):
    batch, seq, d = x.shape
    depth = p.shape[0]
    rows = seq // GRID_W
    assert d == D_MODEL and seq % GRID_W == 0 and rows >= WIN_H and seq % POOL_TM == 0
    m = batch * seq
    x = x.reshape(m, d)
    p = p.reshape(depth, m, p.shape[-1])
    q_scale = jnp.concatenate([jnp.full((d,), HEAD_DIM ** -0.5 * LOG2_E, _F32),
                               jnp.ones((2 * d,), _F32)])

    xg, ssq = x, attn_norm_g[0]
    for i in range(depth):
        j = i // 2
        if i % 2 == 0:
            qkv = mm_qkv(xg, ssq, w_qkv, j, b_qkv[j], q_scale)
            side_layers = list(range(i, min(i + 2, depth)))
            o, side = neighbourhood_attention(qkv, rpb[j], w_down, side_layers, batch, seq)
            w_down_bf16 = dict(zip(side_layers, side))
            x, xg, ssq = mm_residual(o, w_o, j, x, ffn_norm_g[i], tm=1024)
        else:
            x, xg, ssq = multiscale_pool(x, pool_norm_g[j], w_pool, j, pool_scale[j],
                                         ffn_norm_g[i], seq)
        hidden = mm_swiglu(xg, ssq, w_gate, w_up, i)
        x, xg, ssq = mm_residual(hidden, w_down_bf16[i], None, x, ple_norm_g[i])
        ple = functools.partial(mm_ple, xg, ssq, w_ple_gate, b_ple_gate[i], p, w_ple_proj, i, x)
        if i + 1 == depth:
            x = ple(final_norm_g, "final")
        elif (i + 1) % 2 == 0:
            x, xg, ssq = ple(attn_norm_g[(i + 1) // 2], "emit")
        else:
            x = ple(None, "plain")
    return x.reshape(batch, seq, d)
```

```python
import functools

import jax
import jax.numpy as jnp
import numpy as np
from jax import lax
from jax.experimental import pallas as pl
from jax.experimental.pallas import tpu as pltpu

D_MODEL = 2048
GRID_W = 64
N_HEADS = 64
HEAD_DIM = D_MODEL // N_HEADS
WIN_H = 8
WIN_W = 16
POOL_WINDOWS = (2, 4, 8, 16)
GROUP_CH = D_MODEL // len(POOL_WINDOWS)
RMS_EPS = 1e-6

VMEM_LIMIT_BYTES = 56 * 1024 * 1024
LANES = 128
F32_SUBLANES = 8
BF16_SUBLANES = 16

HEADS_PER_GROUP = 8
GROUP_LANES = HEADS_PER_GROUP * HEAD_DIM
KEY_COLS = 32
Q_GROUPS = ((0, 24, 0), (24, 16, 16), (40, 24, 32))
for _q0, _nq, _k0 in Q_GROUPS:
    _cs = np.clip(np.arange(_q0, _q0 + _nq) - WIN_W // 2, 0, GRID_W - WIN_W)
    assert _cs.min() >= _k0 and _cs.max() + WIN_W <= _k0 + KEY_COLS
    assert _k0 % BF16_SUBLANES == 0 and _q0 % F32_SUBLANES == 0 and _nq % F32_SUBLANES == 0
MASK_BIAS = -1e30
LOG2_E = float(np.log2(np.e))

_BF16 = jnp.bfloat16
_F32 = jnp.float32


def _params(*semantics):
    return pltpu.CompilerParams(dimension_semantics=semantics,
                                vmem_limit_bytes=VMEM_LIMIT_BYTES)


def _dot(a, b):
    return jnp.dot(a, b, preferred_element_type=_F32)


def _row_scale(ssq_ref, d):
    total = ssq_ref[:, 0:LANES]
    for part in range(1, ssq_ref.shape[1] // LANES):
        total = total + ssq_ref[:, part * LANES:(part + 1) * LANES]
    return lax.rsqrt(total[:, 0:1] * (1.0 / d) + RMS_EPS)


def _emit_norm_inputs(x, g_ref, xg_ref, ssq_ref):
    xg_ref[...] = (x * g_ref[...]).astype(xg_ref.dtype)
    ssq_ref[...] = jnp.broadcast_to(jnp.sum(x * x, axis=-1, keepdims=True), ssq_ref.shape)


CAST_ROWS = 256


def _cast_weight(w_ref, wb_ref):
    @pl.when(pl.program_id(1) == 0)
    def _():
        def body(c, carry):
            rows = pl.ds(pl.multiple_of(c * CAST_ROWS, CAST_ROWS), CAST_ROWS)
            wb_ref[rows, :] = w_ref[rows, :].astype(wb_ref.dtype)
            return carry
        lax.fori_loop(0, w_ref.shape[0] // CAST_ROWS, body, 0)


def _col_halves(ref):
    half = ref.shape[1] // 2
    return (slice(0, half), slice(half, 2 * half))


ROW_CHUNK = 512


def _row_chunks(ref):
    n = ref.shape[0]
    step = min(n, ROW_CHUNK)
    return [slice(r0, r0 + step) for r0 in range(0, n, step)]


def _emit_norm_cols(x, rows, cols, g_ref, xg_ref):
    xg_ref[rows, cols] = (x * g_ref[:, cols]).astype(xg_ref.dtype)
    return jnp.sum(x * x, axis=-1, keepdims=True)


def _mm_qkv_kernel(d, from_x, a_ref, aux_ref, w_ref, b_ref, s_ref, o_ref, wb_ref):
    _cast_weight(w_ref, wb_ref)
    r_all = None if from_x else _row_scale(aux_ref, d)
    for rows in _row_chunks(o_ref):
        if from_x:
            x = a_ref[rows, :]
            a = (x * aux_ref[...]).astype(_BF16)
            r = lax.rsqrt(jnp.mean(x * x, axis=-1, keepdims=True) + RMS_EPS)
        else:
            a = a_ref[rows, :]
            r = r_all[rows]
        for cols in _col_halves(o_ref):
            acc = _dot(a, wb_ref[:, cols]) * r
            o_ref[rows, cols] = ((acc + b_ref[:, cols]) * s_ref[:, cols]).astype(o_ref.dtype)


def _mm_swiglu_kernel(d, xg_ref, ssq_ref, wg_ref, wu_ref, o_ref, wgb_ref, wub_ref):
    _cast_weight(wg_ref, wgb_ref)
    _cast_weight(wu_ref, wub_ref)
    r_all = _row_scale(ssq_ref, d)
    for rows in _row_chunks(o_ref):
        a = xg_ref[rows, :]
        r = r_all[rows]
        for cols in _col_halves(o_ref):
            gate = _dot(a, wgb_ref[:, cols]) * r
            up = _dot(a, wub_ref[:, cols]) * r
            o_ref[rows, cols] = (gate * jax.nn.sigmoid(gate) * up).astype(o_ref.dtype)


def _mm_residual_kernel(a_ref, w_ref, res_ref, g_ref, x_ref, xg_ref, ssq_ref, *scratch):
    if scratch:
        wb_ref, = scratch
        _cast_weight(w_ref, wb_ref)
    else:
        wb_ref = w_ref
    for rows in _row_chunks(x_ref):
        a = a_ref[rows, :]
        ssq = 0.0
        for cols in _col_halves(x_ref):
            x = res_ref[rows, cols] + _dot(a, wb_ref[:, cols])
            x_ref[rows, cols] = x
            ssq = ssq + _emit_norm_cols(x, rows, cols, g_ref, xg_ref)
        ssq_ref[rows, :] = jnp.broadcast_to(ssq, (ssq.shape[0], ssq_ref.shape[1]))


def _mm_ple_kernel(d, mode, xg_ref, ssq_ref, wg_ref, b_ref, p_ref, wp_ref, res_ref, *rest):
    if mode == "emit":
        g_ref, x_ref, xg_out_ref, ssq_out_ref, wgb_ref, wpb_ref = rest
    elif mode == "final":
        g_ref, x_ref, wgb_ref, wpb_ref = rest
    else:
        x_ref, wgb_ref, wpb_ref = rest
    _cast_weight(wg_ref, wgb_ref)
    _cast_weight(wp_ref, wpb_ref)
    r_all = _row_scale(ssq_ref, d)
    for rows in _row_chunks(x_ref):
        a = xg_ref[rows, :]
        pb = p_ref[rows, :].astype(_BF16)
        r = r_all[rows]
        ssq = 0.0
        for cols in _col_halves(x_ref):
            gate = jax.nn.sigmoid(_dot(a, wgb_ref[:, cols]) * r + b_ref[:, cols])
            x = res_ref[rows, cols] + _dot(pb, wpb_ref[:, cols]) * gate
            x_ref[rows, cols] = x
            if mode == "emit":
                ssq = ssq + _emit_norm_cols(x, rows, cols, g_ref, xg_out_ref)
            elif mode == "final":
                ssq = ssq + jnp.sum(x * x, axis=-1, keepdims=True)
        if mode == "emit":
            ssq_out_ref[rows, :] = jnp.broadcast_to(ssq, (ssq.shape[0], ssq_out_ref.shape[1]))
        elif mode == "final":
            scale = lax.rsqrt(ssq * (1.0 / x_ref.shape[1]) + RMS_EPS)
            x_ref[rows, :] = x_ref[rows, :] * scale * g_ref[...]


def _row_spec(tm, k):
    return pl.BlockSpec((tm, k), lambda j, i: (i, 0))


def _col_spec(layer, k, tn, single_buffer=False):
    mode = pl.Buffered(1) if single_buffer else None
    return pl.BlockSpec((None, k, tn), lambda j, i: (layer, 0, j), pipeline_mode=mode)


def _vec_spec(tn):
    return pl.BlockSpec((1, tn), lambda j, i: (0, j))


def _tile_spec(tm, tn):
    return pl.BlockSpec((tm, tn), lambda j, i: (i, j))


def _ssq_out_spec(tm):
    return pl.BlockSpec((tm, LANES), lambda j, i: (i, j))


def _mm_params():
    return _params("parallel", "arbitrary")


def mm_qkv(a, aux, w, layer, b, s, tm=1024, tn=1024):
    m, k = a.shape
    n = w.shape[-1]
    from_x = a.dtype == _F32
    aux_spec = pl.BlockSpec((1, k), lambda j, i: (0, 0)) if from_x else _row_spec(tm, aux.shape[1])
    aux = aux.reshape(1, k) if from_x else aux
    return pl.pallas_call(
        functools.partial(_mm_qkv_kernel, k, from_x),
        grid=(n // tn, m // tm),
        in_specs=[_row_spec(tm, k), aux_spec, _col_spec(layer, k, tn),
                  _vec_spec(tn), _vec_spec(tn)],
        out_specs=_tile_spec(tm, tn),
        out_shape=jax.ShapeDtypeStruct((m, n), _BF16),
        scratch_shapes=[pltpu.VMEM((k, tn), _BF16)],
        compiler_params=_mm_params(),
        name="mm_qkv",
    )(a, aux, w, b.reshape(1, n), s.reshape(1, n))


def mm_swiglu(xg, ssq, wg, wu, layer, tm=2048, tn=512):
    m, k = xg.shape
    n = wg.shape[-1]
    return pl.pallas_call(
        functools.partial(_mm_swiglu_kernel, k),
        grid=(n // tn, m // tm),
        in_specs=[_row_spec(tm, k), _row_spec(tm, ssq.shape[1]),
                  _col_spec(layer, k, tn), _col_spec(layer, k, tn)],
        out_specs=_tile_spec(tm, tn),
        out_shape=jax.ShapeDtypeStruct((m, n), _BF16),
        scratch_shapes=[pltpu.VMEM((k, tn), _BF16), pltpu.VMEM((k, tn), _BF16)],
        compiler_params=_mm_params(),
        name="mm_swiglu",
    )(xg, ssq, wg, wu)


def _norm_out(m, n, tm, tn):
    specs = [_tile_spec(tm, tn), _tile_spec(tm, tn), _ssq_out_spec(tm)]
    shapes = [jax.ShapeDtypeStruct((m, n), _F32), jax.ShapeDtypeStruct((m, n), _BF16),
              jax.ShapeDtypeStruct((m, LANES * (n // tn)), _F32)]
    return specs, shapes


def mm_residual(a, w, layer, res, g_next, tm=512, tn=1024):
    m, k = a.shape
    n = w.shape[-1]
    out_specs, out_shape = _norm_out(m, n, tm, tn)
    if layer is None:
        w_spec, scratch = pl.BlockSpec((k, tn), lambda j, i: (0, j)), []
    else:
        w_spec, scratch = _col_spec(layer, k, tn), [pltpu.VMEM((k, tn), _BF16)]
    return pl.pallas_call(
        _mm_residual_kernel,
        grid=(n // tn, m // tm),
        in_specs=[_row_spec(tm, k), w_spec, _tile_spec(tm, tn), _vec_spec(tn)],
        out_specs=out_specs,
        out_shape=out_shape,
        scratch_shapes=scratch,
        compiler_params=_mm_params(),
        name="mm_residual",
    )(a, w, res, g_next.reshape(1, n))


def mm_ple(xg, ssq, wg, b, p, wp, layer, res, g, mode, tm=1024, tn=1024):
    m, k = xg.shape
    n = wg.shape[-1]
    kp = p.shape[-1]
    if mode == "final":
        tm, tn = 512, n
    in_specs = [_row_spec(tm, k), _row_spec(tm, ssq.shape[1]),
                _col_spec(layer, k, tn, single_buffer=True), _vec_spec(tn),
                pl.BlockSpec((None, tm, kp), lambda j, i: (layer, i, 0)),
                _col_spec(layer, kp, tn, single_buffer=True), _tile_spec(tm, tn)]
    args = [xg, ssq, wg, b.reshape(1, n), p, wp, res]
    if mode != "plain":
        in_specs.append(_vec_spec(tn))
        args.append(g.reshape(1, n))
    if mode == "emit":
        out_specs, out_shape = _norm_out(m, n, tm, tn)
    else:
        out_specs, out_shape = _tile_spec(tm, tn), jax.ShapeDtypeStruct((m, n), _F32)
    return pl.pallas_call(
        functools.partial(_mm_ple_kernel, k, mode),
        grid=(n // tn, m // tm),
        in_specs=in_specs,
        out_specs=out_specs,
        out_shape=out_shape,
        scratch_shapes=[pltpu.VMEM((k, tn), _BF16), pltpu.VMEM((kp, tn), _BF16)],
        compiler_params=_mm_params(),
        name="mm_ple",
    )(*args)


EXT_PAD = KEY_COLS // 2
N_BIAS_ROWS = 2 * WIN_H - 1
N_QUADS = N_BIAS_ROWS - LANES // KEY_COLS + 1
GROUP_ROWS = HEADS_PER_GROUP * GRID_W
PAIRS_PER_ITER = 5


def _build_bias_quads(ext_ref, quad_ref):
    lane = lax.broadcasted_iota(jnp.int32, (GRID_W, LANES), 1)
    qc = lax.broadcasted_iota(jnp.int32, (GRID_W, LANES), 0)
    lane_j = lane // KEY_COLS
    k0 = jnp.where(qc < Q_GROUPS[1][0], Q_GROUPS[0][2],
                   jnp.where(qc < Q_GROUPS[2][0], Q_GROUPS[1][2], Q_GROUPS[2][2]))
    kc = k0 + lane % KEY_COLS
    c_start = jnp.clip(qc - WIN_W // 2, 0, GRID_W - WIN_W)
    valid = (kc >= c_start) & (kc < c_start + WIN_W)

    def head_body(h, carry):
        def rolled(dr, j):
            row = ext_ref[0, pl.ds(h * N_BIAS_ROWS + dr, 1), :]
            parts = []
            for q0, nq, k0_g in Q_GROUPS:
                shift = (KEY_COLS * j - k0_g - (WIN_W - 1) - EXT_PAD + q0) % LANES
                parts.append(pltpu.roll(jnp.broadcast_to(row, (nq, LANES)), shift, 1,
                                        stride=1, stride_axis=0))
            return jnp.concatenate(parts, axis=0)

        for d0 in range(N_QUADS):
            quad = rolled(d0, 0)
            for j in range(1, LANES // KEY_COLS):
                quad = jnp.where(lane_j == j, rolled(d0 + j, j), quad)
            quad_ref[d0, pl.ds(pl.multiple_of(h * GRID_W, GRID_W), GRID_W), :] = (
                jnp.where(valid, quad * LOG2_E, MASK_BIAS))
        return carry

    lax.fori_loop(0, HEADS_PER_GROUP, head_body, 0)


def _attn_kernel(n_side, q_ref, k_ref, v_ref, ext_ref, *refs):
    side_in, (o_ref, *side_out) = refs[:n_side], refs[n_side:2 * n_side + 1]
    quad_ref, s0_ref, s1_ref, e0_ref, e1_ref, l0_ref, l1_ref = refs[2 * n_side + 1:]
    rows = q_ref.shape[0] // GRID_W
    for wd_ref, wdb_ref in zip(side_in, side_out):
        wdb_ref[...] = wd_ref[...].astype(wdb_ref.dtype)
    lane_head = lax.broadcasted_iota(jnp.int32, (1, GROUP_LANES), 1) // HEAD_DIM
    head_masks = [lane_head == h for h in range(HEADS_PER_GROUP)]

    @pl.when(pl.program_id(1) == 0)
    def _():
        _build_bias_quads(ext_ref, quad_ref)

    def key_rows(ref, r):
        r_start = jnp.clip(r - WIN_H // 2, 0, rows - WIN_H)
        return [ref[pl.ds(pl.multiple_of((r_start + j) * GRID_W, GRID_W), GRID_W), :]
                for j in range(WIN_H)]

    def window(slabs, k0):
        return jnp.concatenate([slab[k0:k0 + KEY_COLS] for slab in slabs], axis=0)

    def score_stage(r, s_ref):
        d0 = jnp.clip(r - WIN_H // 2, 0, rows - WIN_H) - r + (WIN_H - 1)
        q_row = q_ref[pl.ds(pl.multiple_of(r * GRID_W, GRID_W), GRID_W), :].astype(_F32)
        k_slabs = key_rows(k_ref, r)
        off = 0
        for q0, nq, k0 in Q_GROUPS:
            qg = q_row[q0:q0 + nq]
            qm = jnp.concatenate([jnp.where(mask, qg, 0.0) for mask in head_masks],
                                 axis=0).astype(_BF16)
            s = lax.dot_general(qm, window(k_slabs, k0), (((1,), (1,)), ((), ())),
                                preferred_element_type=_F32)
            bias = jnp.concatenate(
                [jnp.concatenate([quad_ref[d0 + half * (LANES // KEY_COLS),
                                           pl.ds(h * GRID_W + q0, nq), :]
                                  for h in range(HEADS_PER_GROUP)], axis=0)
                 for half in range(2)], axis=1)
            s_ref[pl.ds(off, HEADS_PER_GROUP * nq), :] = s + bias
            off += HEADS_PER_GROUP * nq

    def exp_stage(s_ref, e_ref, l_ref):
        off = 0
        for _, nq, _ in Q_GROUPS:
            rows_g = pl.ds(off, HEADS_PER_GROUP * nq)
            s = s_ref[rows_g, :]
            e = jnp.exp2(s - jnp.max(s, axis=-1, keepdims=True))
            e_ref[rows_g, :] = e.astype(e_ref.dtype)
            l_ref[rows_g, :] = jnp.broadcast_to(jnp.sum(e, axis=-1, keepdims=True),
                                                (HEADS_PER_GROUP * nq, LANES))
            off += HEADS_PER_GROUP * nq

    heads_per_vreg = LANES // HEAD_DIM
    half_masks = [head_masks[h][:, :LANES] for h in range(heads_per_vreg)]

    def value_stage(r, e_ref, l_ref):
        outs = []
        v_slabs = key_rows(v_ref, r)
        off = 0
        for q0, nq, k0 in Q_GROUPS:
            rows_g = pl.ds(off, HEADS_PER_GROUP * nq)
            o_all = _dot(e_ref[rows_g, :], window(v_slabs, k0))
            sums = l_ref[rows_g, :]
            o = jnp.where(head_masks[0], o_all[:nq], 0.0)
            for h in range(1, HEADS_PER_GROUP):
                o = o + jnp.where(head_masks[h], o_all[h * nq:(h + 1) * nq], 0.0)
            den = []
            for first in range(0, HEADS_PER_GROUP, heads_per_vreg):
                d = sums[first * nq:(first + 1) * nq]
                for h in range(1, heads_per_vreg):
                    d = jnp.where(half_masks[h], sums[(first + h) * nq:(first + h + 1) * nq], d)
                den.append(d)
            outs.append(o / jnp.concatenate(den, axis=1))
            off += HEADS_PER_GROUP * nq
        o_ref[pl.ds(pl.multiple_of(r * GRID_W, GRID_W), GRID_W), :] = (
            jnp.concatenate(outs, axis=0).astype(o_ref.dtype))

    score_stage(0, s0_ref)
    score_stage(1, s1_ref)
    exp_stage(s0_ref, e0_ref, l0_ref)

    def row_pair(r):
        score_stage(r, s0_ref)
        exp_stage(s1_ref, e1_ref, l1_ref)
        value_stage(r - 2, e0_ref, l0_ref)
        score_stage(r + 1, s1_ref)
        exp_stage(s0_ref, e0_ref, l0_ref)
        value_stage(r - 1, e1_ref, l1_ref)

    n_pairs = rows // 2 - 1
    n_iters = n_pairs // PAIRS_PER_ITER

    def loop_body(k, carry):
        for pair in range(PAIRS_PER_ITER):
            row_pair(2 * (PAIRS_PER_ITER * k + pair) + 2)
        return carry

    lax.fori_loop(0, n_iters, loop_body, 0)
    for pair in range(n_iters * PAIRS_PER_ITER, n_pairs):
        row_pair(2 * pair + 2)
    exp_stage(s1_ref, e1_ref, l1_ref)
    value_stage(rows - 2, e0_ref, l0_ref)
    value_stage(rows - 1, e1_ref, l1_ref)


def _attention_bias_ext(rpb):
    ext = jnp.pad(rpb, ((0, 0), (0, 0), (EXT_PAD, EXT_PAD)), mode="edge")
    ext = jnp.pad(ext, ((0, 0), (0, 0), (0, LANES - ext.shape[-1])))
    return ext.reshape(N_HEADS // HEADS_PER_GROUP, HEADS_PER_GROUP * N_BIAS_ROWS, LANES)


def _cast_share(w, steps):
    k, n = w.shape[1:]
    assert k % (steps * BF16_SUBLANES) == 0
    return (k // steps, n), jax.ShapeDtypeStruct((k, n), _BF16)


def neighbourhood_attention(qkv, rpb, w_side, side_layers, batch, seq):
    n_hg = N_HEADS // HEADS_PER_GROUP
    side_blk, side_shape = _cast_share(w_side, n_hg * batch)
    rows = seq // GRID_W
    assert rows % 2 == 0 and rows >= WIN_H
    blk = (seq, GROUP_LANES)
    ext = _attention_bias_ext(rpb)
    score_scratch = pltpu.VMEM((GROUP_ROWS, WIN_H * KEY_COLS), _F32)
    exp_scratch = pltpu.VMEM((GROUP_ROWS, WIN_H * KEY_COLS), _BF16)
    inv_scratch = pltpu.VMEM((GROUP_ROWS, LANES), _F32)
    o, *side = pl.pallas_call(
        functools.partial(_attn_kernel, len(side_layers)),
        grid=(n_hg, batch),
        in_specs=[pl.BlockSpec(blk, lambda hg, b: (b, hg)),
                  pl.BlockSpec(blk, lambda hg, b: (b, n_hg + hg)),
                  pl.BlockSpec(blk, lambda hg, b: (b, 2 * n_hg + hg)),
                  pl.BlockSpec((1,) + ext.shape[1:], lambda hg, b: (hg, 0, 0))]
                 + [pl.BlockSpec((None,) + side_blk,
                                 functools.partial(lambda layer, hg, b: (layer, hg * batch + b, 0), layer))
                    for layer in side_layers],
        out_specs=[pl.BlockSpec(blk, lambda hg, b: (b, hg))]
                  + [pl.BlockSpec(side_blk, lambda hg, b: (hg * batch + b, 0))] * len(side_layers),
        out_shape=[jax.ShapeDtypeStruct((batch * seq, D_MODEL), _BF16)]
                  + [side_shape] * len(side_layers),
        scratch_shapes=[pltpu.VMEM((N_QUADS, GROUP_ROWS, LANES), _F32),
                        score_scratch, score_scratch, exp_scratch, exp_scratch,
                        inv_scratch, inv_scratch],
        compiler_params=_params("arbitrary", "arbitrary"),
        name="nbr_attention",
    )(qkv, qkv, qkv, ext, *([w_side] * len(side_layers)))
    return o, side


POOL_TM = 512
POOL_HALO = 8


def _pool_kernel(tiles_per_seq, x_ref, prev_ref, next_ref, g_ref, w_ref, scale_ref, gn_ref,
                 o_ref, xg_ref, ssq_ref, hext_ref, mixed_ref, wb_ref):
    i = pl.program_id(0)
    n_ext = POOL_TM + 2 * POOL_HALO
    inner = slice(POOL_HALO, POOL_HALO + POOL_TM)

    def ahead(v, s):
        return pltpu.roll(v, n_ext - s, 0)

    @pl.when(i == 0)
    def _():
        wb_ref[...] = w_ref[...].astype(wb_ref.dtype)

    def rms(v):
        return v * lax.rsqrt(jnp.mean(v * v, axis=-1, keepdims=True) + RMS_EPS) * g_ref[...]

    ti = i % tiles_per_seq
    seq = tiles_per_seq * POOL_TM
    x = x_ref[...]
    h = rms(x)
    has_prev = (ti > 0).astype(_F32)
    has_next = (ti < tiles_per_seq - 1).astype(_F32)
    hext_ref[pl.ds(0, POOL_HALO), :] = rms(prev_ref[...]) * has_prev
    hext_ref[pl.ds(POOL_HALO, POOL_TM), :] = h
    hext_ref[pl.ds(POOL_HALO + POOL_TM, POOL_HALO), :] = rms(next_ref[...]) * has_next
    t = ti * POOL_TM + lax.broadcasted_iota(jnp.int32, (POOL_TM, 1), 0)
    outs = []
    for gi, w in enumerate(POOL_WINDOWS):
        half = w // 2
        assert half <= POOL_HALO
        lo = jnp.clip(t - half, 0, seq)
        hi = jnp.clip(t + w - half, 0, seq)
        inv_cnt = 1.0 / (hi - lo).astype(_F32)
        for c in range(GROUP_CH // LANES):
            cols = slice(gi * GROUP_CH + c * LANES, gi * GROUP_CH + (c + 1) * LANES)
            hx = hext_ref[:, cols]
            run, span = hx, 1
            while span < half:
                run = run + ahead(run, span)
                span *= 2
            first = run if half == POOL_HALO else ahead(run, POOL_HALO - half)
            win = first[:POOL_TM] + run[inner]
            mixed_ref[:, cols] = (win * inv_cnt - hx[inner]).astype(mixed_ref.dtype)
        cols = slice(gi * GROUP_CH, (gi + 1) * GROUP_CH)
        outs.append(x[:, cols] + _dot(mixed_ref[:, cols], wb_ref[gi]) * scale_ref[:, cols])
    x_new = jnp.concatenate(outs, axis=1)
    o_ref[...] = x_new
    _emit_norm_inputs(x_new, gn_ref, xg_ref, ssq_ref)


def multiscale_pool(x, g, w_pool, layer, scale, g_next, seq):
    m, d = x.shape
    tiles_per_seq = seq // POOL_TM
    halo_blocks = POOL_TM // POOL_HALO
    last_halo = m // POOL_HALO - 1
    vec = pl.BlockSpec((1, d), lambda i: (0, 0))
    return pl.pallas_call(
        functools.partial(_pool_kernel, tiles_per_seq),
        grid=(m // POOL_TM,),
        in_specs=[pl.BlockSpec((POOL_TM, d), lambda i: (i, 0)),
                  pl.BlockSpec((POOL_HALO, d), lambda i: (jnp.maximum(i * halo_blocks - 1, 0), 0)),
                  pl.BlockSpec((POOL_HALO, d),
                               lambda i: (jnp.minimum((i + 1) * halo_blocks, last_halo), 0)),
                  vec,
                  pl.BlockSpec((None,) + w_pool.shape[1:], lambda i: (layer, 0, 0, 0)),
                  vec, vec],
        out_specs=[pl.BlockSpec((POOL_TM, d), lambda i: (i, 0)),
                   pl.BlockSpec((POOL_TM, d), lambda i: (i, 0)),
                   pl.BlockSpec((POOL_TM, LANES), lambda i: (i, 0))],
        out_shape=[jax.ShapeDtypeStruct((m, d), _F32), jax.ShapeDtypeStruct((m, d), _BF16),
                   jax.ShapeDtypeStruct((m, LANES), _F32)],
        scratch_shapes=[pltpu.VMEM((POOL_TM + 2 * POOL_HALO, d), _F32),
                        pltpu.VMEM((POOL_TM, d), _BF16),
                        pltpu.VMEM(w_pool.shape[1:], _BF16)],
        compiler_params=_params("arbitrary"),
        name="multiscale_pool",
    )(x, x, x, g.reshape(1, d), w_pool, scale.reshape(1, d), g_next.reshape(1, d))


def kernel(x, p, attn_norm_g, w_qkv, b_qkv, w_o, rpb, pool_norm_g, w_pool, pool_scale,
           ffn_norm_g, w_gate, w_up, w_down, ple_norm_g, w_ple_gate, b_ple_gate,
           w_ple_proj, final_norm_g):
    batch, seq, d = x.shape
    depth = p.shape[0]
    rows = seq // GRID_W
    assert d == D_MODEL and seq % GRID_W == 0 and rows >= WIN_H and seq % POOL_TM == 0
    m = batch * seq
    x = x.reshape(m, d)
    p = p.reshape(depth, m, p.shape[-1])
    q_scale = jnp.concatenate([jnp.full((d,), HEAD_DIM ** -0.5 * LOG2_E, _F32),
                               jnp.ones((2 * d,), _F32)])

    xg, ssq = x, attn_norm_g[0]
    for i in range(depth):
        j = i // 2
        if i % 2 == 0:
            qkv = mm_qkv(xg, ssq, w_qkv, j, b_qkv[j], q_scale)
            side_layers = list(range(i, min(i + 2, depth)))
            o, side = neighbourhood_attention(qkv, rpb[j], w_down, side_layers, batch, seq)
            w_down_bf16 = dict(zip(side_layers, side))
            x, xg, ssq = mm_residual(o, w_o, j, x, ffn_norm_g[i], tm=1024)
        else:
            x, xg, ssq = multiscale_pool(x, pool_norm_g[j], w_pool, j, pool_scale[j],
                                         ffn_norm_g[i], seq)
        hidden = mm_swiglu(xg, ssq, w_gate, w_up, i)
        x, xg, ssq = mm_residual(hidden, w_down_bf16[i], None, x, ple_norm_g[i])
        ple = functools.partial(mm_ple, xg, ssq, w_ple_gate, b_ple_gate[i], p, w_ple_proj, i, x)
        if i + 1 == depth:
            x = ple(final_norm_g, "final")
        elif (i + 1) % 2 == 0:
            x, xg, ssq = ple(attn_norm_g[(i + 1) // 2], "emit")
        else:
            x = ple(None, "plain")
    return x.reshape(batch, seq, d)
```

```python
import functools

import jax
import jax.numpy as jnp
import numpy as np
from jax import lax
from jax.experimental import pallas as pl
from jax.experimental.pallas import tpu as pltpu

D_MODEL = 2048
GRID_W = 64
N_HEADS = 64
HEAD_DIM = D_MODEL // N_HEADS
WIN_H = 8
WIN_W = 16
POOL_WINDOWS = (2, 4, 8, 16)
GROUP_CH = D_MODEL // len(POOL_WINDOWS)
RMS_EPS = 1e-6

VMEM_LIMIT_BYTES = 56 * 1024 * 1024
LANES = 128
F32_SUBLANES = 8
BF16_SUBLANES = 16

HEADS_PER_GROUP = 8
GROUP_LANES = HEADS_PER_GROUP * HEAD_DIM
KEY_COLS = 32
Q_GROUPS = ((0, 24, 0), (24, 16, 16), (40, 24, 32))
for _q0, _nq, _k0 in Q_GROUPS:
    _cs = np.clip(np.arange(_q0, _q0 + _nq) - WIN_W // 2, 0, GRID_W - WIN_W)
    assert _cs.min() >= _k0 and _cs.max() + WIN_W <= _k0 + KEY_COLS
    assert _k0 % BF16_SUBLANES == 0 and _q0 % F32_SUBLANES == 0 and _nq % F32_SUBLANES == 0
MASK_BIAS = -1e30
LOG2_E = float(np.log2(np.e))

_BF16 = jnp.bfloat16
_F32 = jnp.float32


def _params(*semantics):
    return pltpu.CompilerParams(dimension_semantics=semantics,
                                vmem_limit_bytes=VMEM_LIMIT_BYTES)


def _dot(a, b):
    return jnp.dot(a, b, preferred_element_type=_F32)


def _row_scale(ssq_ref, d):
    total = ssq_ref[:, 0:LANES]
    for part in range(1, ssq_ref.shape[1] // LANES):
        total = total + ssq_ref[:, part * LANES:(part + 1) * LANES]
    return lax.rsqrt(total[:, 0:1] * (1.0 / d) + RMS_EPS)


def _emit_norm_inputs(x, g_ref, xg_ref, ssq_ref):
    xg_ref[...] = (x * g_ref[...]).astype(xg_ref.dtype)
    ssq_ref[...] = jnp.broadcast_to(jnp.sum(x * x, axis=-1, keepdims=True), ssq_ref.shape)


CAST_ROWS = 256


def _cast_weight(w_ref, wb_ref):
    @pl.when(pl.program_id(1) == 0)
    def _():
        def body(c, carry):
            rows = pl.ds(pl.multiple_of(c * CAST_ROWS, CAST_ROWS), CAST_ROWS)
            wb_ref[rows, :] = w_ref[rows, :].astype(wb_ref.dtype)
            return carry
        lax.fori_loop(0, w_ref.shape[0] // CAST_ROWS, body, 0)


def _col_halves(ref):
    half = ref.shape[1] // 2
    return (slice(0, half), slice(half, 2 * half))


ROW_CHUNK = 512


def _row_chunks(ref):
    n = ref.shape[0]
    step = min(n, ROW_CHUNK)
    return [slice(r0, r0 + step) for r0 in range(0, n, step)]


def _emit_norm_cols(x, rows, cols, g_ref, xg_ref):
    xg_ref[rows, cols] = (x * g_ref[:, cols]).astype(xg_ref.dtype)
    return jnp.sum(x * x, axis=-1, keepdims=True)


def _mm_qkv_kernel(d, from_x, a_ref, aux_ref, w_ref, b_ref, s_ref, o_ref, wb_ref):
    _cast_weight(w_ref, wb_ref)
    r_all = None if from_x else _row_scale(aux_ref, d)
    for rows in _row_chunks(o_ref):
        if from_x:
            x = a_ref[rows, :]
            a = (x * aux_ref[...]).astype(_BF16)
            r = lax.rsqrt(jnp.mean(x * x, axis=-1, keepdims=True) + RMS_EPS)
        else:
            a = a_ref[rows, :]
            r = r_all[rows]
        for cols in _col_halves(o_ref):
            acc = _dot(a, wb_ref[:, cols]) * r
            o_ref[rows, cols] = ((acc + b_ref[:, cols]) * s_ref[:, cols]).astype(o_ref.dtype)


def _mm_swiglu_kernel(d, xg_ref, ssq_ref, wg_ref, wu_ref, o_ref, wgb_ref, wub_ref):
    _cast_weight(wg_ref, wgb_ref)
    _cast_weight(wu_ref, wub_ref)
    r_all = _row_scale(ssq_ref, d)
    for rows in _row_chunks(o_ref):
        a = xg_ref[rows, :]
        r = r_all[rows]
        for cols in _col_halves(o_ref):
            gate = _dot(a, wgb_ref[:, cols]) * r
            up = _dot(a, wub_ref[:, cols]) * r
            o_ref[rows, cols] = (gate * jax.nn.sigmoid(gate) * up).astype(o_ref.dtype)


def _mm_residual_kernel(a_ref, w_ref, res_ref, g_ref, x_ref, xg_ref, ssq_ref, *scratch):
    if scratch:
        wb_ref, = scratch
        _cast_weight(w_ref, wb_ref)
    else:
        wb_ref = w_ref
    for rows in _row_chunks(x_ref):
        a = a_ref[rows, :]
        ssq = 0.0
        for cols in _col_halves(x_ref):
            x = res_ref[rows, cols] + _dot(a, wb_ref[:, cols])
            x_ref[rows, cols] = x
            ssq = ssq + _emit_norm_cols(x, rows, cols, g_ref, xg_ref)
        ssq_ref[rows, :] = jnp.broadcast_to(ssq, (ssq.shape[0], ssq_ref.shape[1]))


def _mm_ple_kernel(d, mode, xg_ref, ssq_ref, wg_ref, b_ref, p_ref, wp_ref, res_ref, *rest):
    if mode == "emit":
        g_ref, x_ref, xg_out_ref, ssq_out_ref, wgb_ref, wpb_ref = rest
    elif mode == "final":
        g_ref, x_ref, wgb_ref, wpb_ref = rest
    else:
        x_ref, wgb_ref, wpb_ref = rest
    _cast_weight(wg_ref, wgb_ref)
    _cast_weight(wp_ref, wpb_ref)
    r_all = _row_scale(ssq_ref, d)
    for rows in _row_chunks(x_ref):
        a = xg_ref[rows, :]
        pb = p_ref[rows, :].astype(_BF16)
        r = r_all[rows]
        ssq = 0.0
        for cols in _col_halves(x_ref):
            gate = jax.nn.sigmoid(_dot(a, wgb_ref[:, cols]) * r + b_ref[:, cols])
            x = res_ref[rows, cols] + _dot(pb, wpb_ref[:, cols]) * gate
            x_ref[rows, cols] = x
            if mode == "emit":
                ssq = ssq + _emit_norm_cols(x, rows, cols, g_ref, xg_out_ref)
            elif mode == "final":
                ssq = ssq + jnp.sum(x * x, axis=-1, keepdims=True)
        if mode == "emit":
            ssq_out_ref[rows, :] = jnp.broadcast_to(ssq, (ssq.shape[0], ssq_out_ref.shape[1]))
        elif mode == "final":
            scale = lax.rsqrt(ssq * (1.0 / x_ref.shape[1]) + RMS_EPS)
            x_ref[rows, :] = x_ref[rows, :] * scale * g_ref[...]


def _row_spec(tm, k):
    return pl.BlockSpec((tm, k), lambda j, i: (i, 0))


def _col_spec(layer, k, tn, single_buffer=False):
    mode = pl.Buffered(1) if single_buffer else None
    return pl.BlockSpec((None, k, tn), lambda j, i: (layer, 0, j), pipeline_mode=mode)


def _vec_spec(tn):
    return pl.BlockSpec((1, tn), lambda j, i: (0, j))


def _tile_spec(tm, tn):
    return pl.BlockSpec((tm, tn), lambda j, i: (i, j))


def _ssq_out_spec(tm):
    return pl.BlockSpec((tm, LANES), lambda j, i: (i, j))


def _mm_params():
    return _params("parallel", "arbitrary")


def mm_qkv(a, aux, w, layer, b, s, tm=1024, tn=1536):
    m, k = a.shape
    n = w.shape[-1]
    from_x = a.dtype == _F32
    aux_spec = pl.BlockSpec((1, k), lambda j, i: (0, 0)) if from_x else _row_spec(tm, aux.shape[1])
    aux = aux.reshape(1, k) if from_x else aux
    return pl.pallas_call(
        functools.partial(_mm_qkv_kernel, k, from_x),
        grid=(n // tn, m // tm),
        in_specs=[_row_spec(tm, k), aux_spec, _col_spec(layer, k, tn),
                  _vec_spec(tn), _vec_spec(tn)],
        out_specs=_tile_spec(tm, tn),
        out_shape=jax.ShapeDtypeStruct((m, n), _BF16),
        scratch_shapes=[pltpu.VMEM((k, tn), _BF16)],
        compiler_params=_mm_params(),
        name="mm_qkv",
    )(a, aux, w, b.reshape(1, n), s.reshape(1, n))


def mm_swiglu(xg, ssq, wg, wu, layer, tm=2048, tn=512):
    m, k = xg.shape
    n = wg.shape[-1]
    return pl.pallas_call(
        functools.partial(_mm_swiglu_kernel, k),
        grid=(n // tn, m // tm),
        in_specs=[_row_spec(tm, k), _row_spec(tm, ssq.shape[1]),
                  _col_spec(layer, k, tn), _col_spec(layer, k, tn)],
        out_specs=_tile_spec(tm, tn),
        out_shape=jax.ShapeDtypeStruct((m, n), _BF16),
        scratch_shapes=[pltpu.VMEM((k, tn), _BF16), pltpu.VMEM((k, tn), _BF16)],
        compiler_params=_mm_params(),
        name="mm_swiglu",
    )(xg, ssq, wg, wu)


def _norm_out(m, n, tm, tn):
    specs = [_tile_spec(tm, tn), _tile_spec(tm, tn), _ssq_out_spec(tm)]
    shapes = [jax.ShapeDtypeStruct((m, n), _F32), jax.ShapeDtypeStruct((m, n), _BF16),
              jax.ShapeDtypeStruct((m, LANES * (n // tn)), _F32)]
    return specs, shapes


def mm_residual(a, w, layer, res, g_next, tm=512, tn=1024):
    m, k = a.shape
    n = w.shape[-1]
    out_specs, out_shape = _norm_out(m, n, tm, tn)
    if layer is None:
        w_spec, scratch = pl.BlockSpec((k, tn), lambda j, i: (0, j)), []
    else:
        w_spec, scratch = _col_spec(layer, k, tn), [pltpu.VMEM((k, tn), _BF16)]
    return pl.pallas_call(
        _mm_residual_kernel,
        grid=(n // tn, m // tm),
        in_specs=[_row_spec(tm, k), w_spec, _tile_spec(tm, tn), _vec_spec(tn)],
        out_specs=out_specs,
        out_shape=out_shape,
        scratch_shapes=scratch,
        compiler_params=_mm_params(),
        name="mm_residual",
    )(a, w, res, g_next.reshape(1, n))


def mm_ple(xg, ssq, wg, b, p, wp, layer, res, g, mode, tm=1024, tn=1024):
    m, k = xg.shape
    n = wg.shape[-1]
    kp = p.shape[-1]
    if mode == "final":
        tm, tn = 512, n
    in_specs = [_row_spec(tm, k), _row_spec(tm, ssq.shape[1]),
                _col_spec(layer, k, tn, single_buffer=True), _vec_spec(tn),
                pl.BlockSpec((None, tm, kp), lambda j, i: (layer, i, 0)),
                _col_spec(layer, kp, tn, single_buffer=True), _tile_spec(tm, tn)]
    args = [xg, ssq, wg, b.reshape(1, n), p, wp, res]
    if mode != "plain":
        in_specs.append(_vec_spec(tn))
        args.append(g.reshape(1, n))
    if mode == "emit":
        out_specs, out_shape = _norm_out(m, n, tm, tn)
    else:
        out_specs, out_shape = _tile_spec(tm, tn), jax.ShapeDtypeStruct((m, n), _F32)
    return pl.pallas_call(
        functools.partial(_mm_ple_kernel, k, mode),
        grid=(n // tn, m // tm),
        in_specs=in_specs,
        out_specs=out_specs,
        out_shape=out_shape,
        scratch_shapes=[pltpu.VMEM((k, tn), _BF16), pltpu.VMEM((kp, tn), _BF16)],
        compiler_params=_mm_params(),
        name="mm_ple",
    )(*args)


EXT_PAD = KEY_COLS // 2
N_BIAS_ROWS = 2 * WIN_H - 1
N_QUADS = N_BIAS_ROWS - LANES // KEY_COLS + 1
GROUP_ROWS = HEADS_PER_GROUP * GRID_W
PAIRS_PER_ITER = 4


def _build_bias_quads(ext_ref, quad_ref):
    lane = lax.broadcasted_iota(jnp.int32, (GRID_W, LANES), 1)
    qc = lax.broadcasted_iota(jnp.int32, (GRID_W, LANES), 0)
    lane_j = lane // KEY_COLS
    k0 = jnp.where(qc < Q_GROUPS[1][0], Q_GROUPS[0][2],
                   jnp.where(qc < Q_GROUPS[2][0], Q_GROUPS[1][2], Q_GROUPS[2][2]))
    kc = k0 + lane % KEY_COLS
    c_start = jnp.clip(qc - WIN_W // 2, 0, GRID_W - WIN_W)
    valid = (kc >= c_start) & (kc < c_start + WIN_W)

    def head_body(h, carry):
        def rolled(dr, j):
            row = ext_ref[0, pl.ds(h * N_BIAS_ROWS + dr, 1), :]
            parts = []
            for q0, nq, k0_g in Q_GROUPS:
                shift = (KEY_COLS * j - k0_g - (WIN_W - 1) - EXT_PAD + q0) % LANES
                parts.append(pltpu.roll(jnp.broadcast_to(row, (nq, LANES)), shift, 1,
                                        stride=1, stride_axis=0))
            return jnp.concatenate(parts, axis=0)

        for d0 in range(N_QUADS):
            quad = rolled(d0, 0)
            for j in range(1, LANES // KEY_COLS):
                quad = jnp.where(lane_j == j, rolled(d0 + j, j), quad)
            quad_ref[d0, pl.ds(pl.multiple_of(h * GRID_W, GRID_W), GRID_W), :] = (
                jnp.where(valid, quad * LOG2_E, MASK_BIAS))
        return carry

    lax.fori_loop(0, HEADS_PER_GROUP, head_body, 0)


def _attn_kernel(n_side, q_ref, k_ref, v_ref, ext_ref, *refs):
    side_in, (o_ref, *side_out) = refs[:n_side], refs[n_side:2 * n_side + 1]
    quad_ref, s0_ref, s1_ref, e0_ref, e1_ref, l0_ref, l1_ref = refs[2 * n_side + 1:]
    rows = q_ref.shape[0] // GRID_W
    for wd_ref, wdb_ref in zip(side_in, side_out):
        wdb_ref[...] = wd_ref[...].astype(wdb_ref.dtype)
    lane_head = lax.broadcasted_iota(jnp.int32, (1, GROUP_LANES), 1) // HEAD_DIM
    head_masks = [lane_head == h for h in range(HEADS_PER_GROUP)]

    @pl.when(pl.program_id(1) == 0)
    def _():
        _build_bias_quads(ext_ref, quad_ref)

    def key_rows(ref, r):
        r_start = jnp.clip(r - WIN_H // 2, 0, rows - WIN_H)
        return [ref[pl.ds(pl.multiple_of((r_start + j) * GRID_W, GRID_W), GRID_W), :]
                for j in range(WIN_H)]

    def window(slabs, k0):
        return jnp.concatenate([slab[k0:k0 + KEY_COLS] for slab in slabs], axis=0)

    def score_stage(r, s_ref):
        d0 = jnp.clip(r - WIN_H // 2, 0, rows - WIN_H) - r + (WIN_H - 1)
        q_row = q_ref[pl.ds(pl.multiple_of(r * GRID_W, GRID_W), GRID_W), :].astype(_F32)
        k_slabs = key_rows(k_ref, r)
        off = 0
        for q0, nq, k0 in Q_GROUPS:
            qg = q_row[q0:q0 + nq]
            qm = jnp.concatenate([jnp.where(mask, qg, 0.0) for mask in head_masks],
                                 axis=0).astype(_BF16)
            s = lax.dot_general(qm, window(k_slabs, k0), (((1,), (1,)), ((), ())),
                                preferred_element_type=_F32)
            bias = jnp.concatenate(
                [jnp.concatenate([quad_ref[d0 + half * (LANES // KEY_COLS),
                                           pl.ds(h * GRID_W + q0, nq), :]
                                  for h in range(HEADS_PER_GROUP)], axis=0)
                 for half in range(2)], axis=1)
            s_ref[pl.ds(off, HEADS_PER_GROUP * nq), :] = s + bias
            off += HEADS_PER_GROUP * nq

    def exp_stage(s_ref, e_ref, l_ref):
        off = 0
        for _, nq, _ in Q_GROUPS:
            rows_g = pl.ds(off, HEADS_PER_GROUP * nq)
            s = s_ref[rows_g, :]
            e = jnp.exp2(s - jnp.max(s, axis=-1, keepdims=True))
            e_ref[rows_g, :] = e.astype(e_ref.dtype)
            l_ref[rows_g, :] = jnp.broadcast_to(jnp.sum(e, axis=-1, keepdims=True),
                                                (HEADS_PER_GROUP * nq, LANES))
            off += HEADS_PER_GROUP * nq

    heads_per_vreg = LANES // HEAD_DIM
    half_masks = [head_masks[h][:, :LANES] for h in range(heads_per_vreg)]

    def value_stage(r, e_ref, l_ref):
        outs = []
        v_slabs = key_rows(v_ref, r)
        off = 0
        for q0, nq, k0 in Q_GROUPS:
            rows_g = pl.ds(off, HEADS_PER_GROUP * nq)
            o_all = _dot(e_ref[rows_g, :], window(v_slabs, k0))
            sums = l_ref[rows_g, :]
            o = jnp.where(head_masks[0], o_all[:nq], 0.0)
            for h in range(1, HEADS_PER_GROUP):
                o = o + jnp.where(head_masks[h], o_all[h * nq:(h + 1) * nq], 0.0)
            den = []
            for first in range(0, HEADS_PER_GROUP, heads_per_vreg):
                d = sums[first * nq:(first + 1) * nq]
                for h in range(1, heads_per_vreg):
                    d = jnp.where(half_masks[h], sums[(first + h) * nq:(first + h + 1) * nq], d)
                den.append(d)
            outs.append(o / jnp.concatenate(den, axis=1))
            off += HEADS_PER_GROUP * nq
        o_ref[pl.ds(pl.multiple_of(r * GRID_W, GRID_W), GRID_W), :] = (
            jnp.concatenate(outs, axis=0).astype(o_ref.dtype))

    score_stage(0, s0_ref)
    score_stage(1, s1_ref)
    exp_stage(s0_ref, e0_ref, l0_ref)

    def row_pair(r):
        score_stage(r, s0_ref)
        exp_stage(s1_ref, e1_ref, l1_ref)
        value_stage(r - 2, e0_ref, l0_ref)
        score_stage(r + 1, s1_ref)
        exp_stage(s0_ref, e0_ref, l0_ref)
        value_stage(r - 1, e1_ref, l1_ref)

    n_pairs = rows // 2 - 1
    n_iters = n_pairs // PAIRS_PER_ITER

    def loop_body(k, carry):
        for pair in range(PAIRS_PER_ITER):
            row_pair(2 * (PAIRS_PER_ITER * k + pair) + 2)
        return carry

    lax.fori_loop(0, n_iters, loop_body, 0)
    for pair in range(n_iters * PAIRS_PER_ITER, n_pairs):
        row_pair(2 * pair + 2)
    exp_stage(s1_ref, e1_ref, l1_ref)
    value_stage(rows - 2, e0_ref, l0_ref)
    value_stage(rows - 1, e1_ref, l1_ref)


def _attention_bias_ext(rpb):
    ext = jnp.pad(rpb, ((0, 0), (0, 0), (EXT_PAD, EXT_PAD)), mode="edge")
    ext = jnp.pad(ext, ((0, 0), (0, 0), (0, LANES - ext.shape[-1])))
    return ext.reshape(N_HEADS // HEADS_PER_GROUP, HEADS_PER_GROUP * N_BIAS_ROWS, LANES)


def _cast_share(w, steps):
    k, n = w.shape[1:]
    assert k % (steps * BF16_SUBLANES) == 0
    return (k // steps, n), jax.ShapeDtypeStruct((k, n), _BF16)


def neighbourhood_attention(qkv, rpb, w_side, side_layers, batch, seq):
    n_hg = N_HEADS // HEADS_PER_GROUP
    side_blk, side_shape = _cast_share(w_side, n_hg * batch)
    rows = seq // GRID_W
    assert rows % 2 == 0 and rows >= WIN_H
    blk = (seq, GROUP_LANES)
    ext = _attention_bias_ext(rpb)
    score_scratch = pltpu.VMEM((GROUP_ROWS, WIN_H * KEY_COLS), _F32)
    exp_scratch = pltpu.VMEM((GROUP_ROWS, WIN_H * KEY_COLS), _BF16)
    inv_scratch = pltpu.VMEM((GROUP_ROWS, LANES), _F32)
    o, *side = pl.pallas_call(
        functools.partial(_attn_kernel, len(side_layers)),
        grid=(n_hg, batch),
        in_specs=[pl.BlockSpec(blk, lambda hg, b: (b, hg)),
                  pl.BlockSpec(blk, lambda hg, b: (b, n_hg + hg)),
                  pl.BlockSpec(blk, lambda hg, b: (b, 2 * n_hg + hg)),
                  pl.BlockSpec((1,) + ext.shape[1:], lambda hg, b: (hg, 0, 0))]
                 + [pl.BlockSpec((None,) + side_blk,
                                 functools.partial(lambda layer, hg, b: (layer, hg * batch + b, 0), layer))
                    for layer in side_layers],
        out_specs=[pl.BlockSpec(blk, lambda hg, b: (b, hg))]
                  + [pl.BlockSpec(side_blk, lambda hg, b: (hg * batch + b, 0))] * len(side_layers),
        out_shape=[jax.ShapeDtypeStruct((batch * seq, D_MODEL), _BF16)]
                  + [side_shape] * len(side_layers),
        scratch_shapes=[pltpu.VMEM((N_QUADS, GROUP_ROWS, LANES), _F32),
                        score_scratch, score_scratch, exp_scratch, exp_scratch,
                        inv_scratch, inv_scratch],
        compiler_params=_params("arbitrary", "arbitrary"),
        name="nbr_attention",
    )(qkv, qkv, qkv, ext, *([w_side] * len(side_layers)))
    return o, side


POOL_TM = 512
POOL_HALO = 8


def _pool_kernel(tiles_per_seq, x_ref, prev_ref, next_ref, g_ref, w_ref, scale_ref, gn_ref,
                 o_ref, xg_ref, ssq_ref, hext_ref, mixed_ref, wb_ref):
    i = pl.program_id(0)
    n_ext = POOL_TM + 2 * POOL_HALO
    inner = slice(POOL_HALO, POOL_HALO + POOL_TM)

    def ahead(v, s):
        return pltpu.roll(v, n_ext - s, 0)

    @pl.when(i == 0)
    def _():
        wb_ref[...] = w_ref[...].astype(wb_ref.dtype)

    def rms(v):
        return v * lax.rsqrt(jnp.mean(v * v, axis=-1, keepdims=True) + RMS_EPS) * g_ref[...]

    ti = i % tiles_per_seq
    seq = tiles_per_seq * POOL_TM
    x = x_ref[...]
    h = rms(x)
    has_prev = (ti > 0).astype(_F32)
    has_next = (ti < tiles_per_seq - 1).astype(_F32)
    hext_ref[pl.ds(0, POOL_HALO), :] = rms(prev_ref[...]) * has_prev
    hext_ref[pl.ds(POOL_HALO, POOL_TM), :] = h
    hext_ref[pl.ds(POOL_HALO + POOL_TM, POOL_HALO), :] = rms(next_ref[...]) * has_next
    t = ti * POOL_TM + lax.broadcasted_iota(jnp.int32, (POOL_TM, 1), 0)
    outs = []
    for gi, w in enumerate(POOL_WINDOWS):
        half = w // 2
        assert half <= POOL_HALO
        lo = jnp.clip(t - half, 0, seq)
        hi = jnp.clip(t + w - half, 0, seq)
        inv_cnt = 1.0 / (hi - lo).astype(_F32)
        for c in range(GROUP_CH // LANES):
            cols = slice(gi * GROUP_CH + c * LANES, gi * GROUP_CH + (c + 1) * LANES)
            hx = hext_ref[:, cols]
            run, span = hx, 1
            while span < half:
                run = run + ahead(run, span)
                span *= 2
            first = run if half == POOL_HALO else ahead(run, POOL_HALO - half)
            win = first[:POOL_TM] + run[inner]
            mixed_ref[:, cols] = (win * inv_cnt - hx[inner]).astype(mixed_ref.dtype)
        cols = slice(gi * GROUP_CH, (gi + 1) * GROUP_CH)
        outs.append(x[:, cols] + _dot(mixed_ref[:, cols], wb_ref[gi]) * scale_ref[:, cols])
    x_new = jnp.concatenate(outs, axis=1)
    o_ref[...] = x_new
    _emit_norm_inputs(x_new, gn_ref, xg_ref, ssq_ref)


def multiscale_pool(x, g, w_pool, layer, scale, g_next, seq):
    m, d = x.shape
    tiles_per_seq = seq // POOL_TM
    halo_blocks = POOL_TM // POOL_HALO
    last_halo = m // POOL_HALO - 1
    vec = pl.BlockSpec((1, d), lambda i: (0, 0))
    return pl.pallas_call(
        functools.partial(_pool_kernel, tiles_per_seq),
        grid=(m // POOL_TM,),
        in_specs=[pl.BlockSpec((POOL_TM, d), lambda i: (i, 0)),
                  pl.BlockSpec((POOL_HALO, d), lambda i: (jnp.maximum(i * halo_blocks - 1, 0), 0)),
                  pl.BlockSpec((POOL_HALO, d),
                               lambda i: (jnp.minimum((i + 1) * halo_blocks, last_halo), 0)),
                  vec,
                  pl.BlockSpec((None,) + w_pool.shape[1:], lambda i: (layer, 0, 0, 0)),
                  vec, vec],
        out_specs=[pl.BlockSpec((POOL_TM, d), lambda i: (i, 0)),
                   pl.BlockSpec((POOL_TM, d), lambda i: (i, 0)),
                   pl.BlockSpec((POOL_TM, LANES), lambda i: (i, 0))],
        out_shape=[jax.ShapeDtypeStruct((m, d), _F32), jax.ShapeDtypeStruct((m, d), _BF16),
                   jax.ShapeDtypeStruct((m, LANES), _F32)],
        scratch_shapes=[pltpu.VMEM((POOL_TM + 2 * POOL_HALO, d), _F32),
                        pltpu.VMEM((POOL_TM, d), _BF16),
                        pltpu.VMEM(w_pool.shape[1:], _BF16)],
        compiler_params=_params("arbitrary"),
        name="multiscale_pool",
    )(x, x, x, g.reshape(1, d), w_pool, scale.reshape(1, d), g_next.reshape(1, d))


def kernel(x, p, attn_norm_g, w_qkv, b_qkv, w_o, rpb, pool_norm_g, w_pool, pool_scale,
           ffn_norm_g, w_gate, w_up, w_down, ple_norm_g, w_ple_gate, b_ple_gate,
           w_ple_proj, final_norm_g):
    batch, seq, d = x.shape
    depth = p.shape[0]
    rows = seq // GRID_W
    assert d == D_MODEL and seq % GRID_W == 0 and rows >= WIN_H and seq % POOL_TM == 0
    m = batch * seq
    x = x.reshape(m, d)
    p = p.reshape(depth, m, p.shape[-1])
    q_scale = jnp.concatenate([jnp.full((d,), HEAD_DIM ** -0.5 * LOG2_E, _F32),
                               jnp.ones((2 * d,), _F32)])

    xg, ssq = x, attn_norm_g[0]
    for i in range(depth):
        j = i // 2
        if i % 2 == 0:
            qkv = mm_qkv(xg, ssq, w_qkv, j, b_qkv[j], q_scale)
            side_layers = list(range(i, min(i + 2, depth)))
            o, side = neighbourhood_attention(qkv, rpb[j], w_down, side_layers, batch, seq)
            w_down_bf16 = dict(zip(side_layers, side))
            x, xg, ssq = mm_residual(o, w_o, j, x, ffn_norm_g[i], tm=1024)
        else:
            x, xg, ssq = multiscale_pool(x, pool_norm_g[j], w_pool, j, pool_scale[j],
                                         ffn_norm_g[i], seq)
        hidden = mm_swiglu(xg, ssq, w_gate, w_up, i)
        x, xg, ssq = mm_residual(hidden, w_down_bf16[i], None, x, ple_norm_g[i])
        ple = functools.partial(mm_ple, xg, ssq, w_ple_gate, b_ple_gate[i], p, w_ple_proj, i, x)
        if i + 1 == depth:
            x = ple(final_norm_g, "final")
        elif (i + 1) % 2 == 0:
            x, xg, ssq = ple(attn_norm_g[(i + 1) // 2], "emit")
        else:
            x = ple(None, "plain")
    return x.reshape(batch, seq, d)
```

```python
import functools

import jax
import jax.numpy as jnp
import numpy as np
from jax import lax
from jax.experimental import pallas as pl
from jax.experimental.pallas import tpu as pltpu

D_MODEL = 2048
GRID_W = 64
N_HEADS = 64
HEAD_DIM = D_MODEL // N_HEADS
WIN_H = 8
WIN_W = 16
POOL_WINDOWS = (2, 4, 8, 16)
GROUP_CH = D_MODEL // len(POOL_WINDOWS)
RMS_EPS = 1e-6

VMEM_LIMIT_BYTES = 56 * 1024 * 1024
LANES = 128
F32_SUBLANES = 8
BF16_SUBLANES = 16

HEADS_PER_GROUP = 8
GROUP_LANES = HEADS_PER_GROUP * HEAD_DIM
KEY_COLS = 32
Q_GROUPS = ((0, 24, 0), (24, 16, 16), (40, 24, 32))
for _q0, _nq, _k0 in Q_GROUPS:
    _cs = np.clip(np.arange(_q0, _q0 + _nq) - WIN_W // 2, 0, GRID_W - WIN_W)
    assert _cs.min() >= _k0 and _cs.max() + WIN_W <= _k0 + KEY_COLS
    assert _k0 % BF16_SUBLANES == 0 and _q0 % F32_SUBLANES == 0 and _nq % F32_SUBLANES == 0
MASK_BIAS = -1e30
LOG2_E = float(np.log2(np.e))

_BF16 = jnp.bfloat16
_F32 = jnp.float32


def _params(*semantics):
    return pltpu.CompilerParams(dimension_semantics=semantics,
                                vmem_limit_bytes=VMEM_LIMIT_BYTES)


def _dot(a, b):
    return jnp.dot(a, b, preferred_element_type=_F32)


def _row_scale(ssq_ref, d):
    total = ssq_ref[:, 0:LANES]
    for part in range(1, ssq_ref.shape[1] // LANES):
        total = total + ssq_ref[:, part * LANES:(part + 1) * LANES]
    return lax.rsqrt(total[:, 0:1] * (1.0 / d) + RMS_EPS)


def _emit_norm_inputs(x, g_ref, xg_ref, ssq_ref):
    xg_ref[...] = (x * g_ref[...]).astype(xg_ref.dtype)
    ssq_ref[...] = jnp.broadcast_to(jnp.sum(x * x, axis=-1, keepdims=True), ssq_ref.shape)


CAST_ROWS = 256


def _cast_weight(w_ref, wb_ref):
    @pl.when(pl.program_id(1) == 0)
    def _():
        def body(c, carry):
            rows = pl.ds(pl.multiple_of(c * CAST_ROWS, CAST_ROWS), CAST_ROWS)
            wb_ref[rows, :] = w_ref[rows, :].astype(wb_ref.dtype)
            return carry
        lax.fori_loop(0, w_ref.shape[0] // CAST_ROWS, body, 0)


def _col_halves(ref):
    half = ref.shape[1] // 2
    return (slice(0, half), slice(half, 2 * half))


ROW_CHUNK = 512


def _row_chunks(ref):
    n = ref.shape[0]
    step = min(n, ROW_CHUNK)
    return [slice(r0, r0 + step) for r0 in range(0, n, step)]


def _emit_norm_cols(x, rows, cols, g_ref, xg_ref):
    xg_ref[rows, cols] = (x * g_ref[:, cols]).astype(xg_ref.dtype)
    return jnp.sum(x * x, axis=-1, keepdims=True)


def _mm_qkv_kernel(d, from_x, a_ref, aux_ref, w_ref, b_ref, s_ref, o_ref, wb_ref):
    _cast_weight(w_ref, wb_ref)
    r_all = None if from_x else _row_scale(aux_ref, d)
    for rows in _row_chunks(o_ref):
        if from_x:
            x = a_ref[rows, :]
            a = (x * aux_ref[...]).astype(_BF16)
            r = lax.rsqrt(jnp.mean(x * x, axis=-1, keepdims=True) + RMS_EPS)
        else:
            a = a_ref[rows, :]
            r = r_all[rows]
        for cols in _col_halves(o_ref):
            acc = _dot(a, wb_ref[:, cols]) * r
            o_ref[rows, cols] = ((acc + b_ref[:, cols]) * s_ref[:, cols]).astype(o_ref.dtype)


def _mm_swiglu_kernel(d, xg_ref, ssq_ref, wg_ref, wu_ref, o_ref, wgb_ref, wub_ref):
    _cast_weight(wg_ref, wgb_ref)
    _cast_weight(wu_ref, wub_ref)
    r_all = _row_scale(ssq_ref, d)
    for rows in _row_chunks(o_ref):
        a = xg_ref[rows, :]
        r = r_all[rows]
        for cols in _col_halves(o_ref):
            gate = _dot(a, wgb_ref[:, cols]) * r
            up = _dot(a, wub_ref[:, cols]) * r
            o_ref[rows, cols] = (gate * jax.nn.sigmoid(gate) * up).astype(o_ref.dtype)


def _mm_residual_kernel(a_ref, w_ref, res_ref, g_ref, x_ref, xg_ref, ssq_ref, *scratch):
    if scratch:
        wb_ref, = scratch
        _cast_weight(w_ref, wb_ref)
    else:
        wb_ref = w_ref
    for rows in _row_chunks(x_ref):
        a = a_ref[rows, :]
        ssq = 0.0
        for cols in _col_halves(x_ref):
            x = res_ref[rows, cols] + _dot(a, wb_ref[:, cols])
            x_ref[rows, cols] = x
            ssq = ssq + _emit_norm_cols(x, rows, cols, g_ref, xg_ref)
        ssq_ref[rows, :] = jnp.broadcast_to(ssq, (ssq.shape[0], ssq_ref.shape[1]))


def _mm_ple_kernel(d, mode, xg_ref, ssq_ref, wg_ref, b_ref, p_ref, wp_ref, res_ref, *rest):
    if mode == "emit":
        g_ref, x_ref, xg_out_ref, ssq_out_ref, wgb_ref, wpb_ref = rest
    elif mode == "final":
        g_ref, x_ref, wgb_ref, wpb_ref = rest
    else:
        x_ref, wgb_ref, wpb_ref = rest
    _cast_weight(wg_ref, wgb_ref)
    _cast_weight(wp_ref, wpb_ref)
    r_all = _row_scale(ssq_ref, d)
    for rows in _row_chunks(x_ref):
        a = xg_ref[rows, :]
        pb = p_ref[rows, :].astype(_BF16)
        r = r_all[rows]
        ssq = 0.0
        for cols in _col_halves(x_ref):
            gate = jax.nn.sigmoid(_dot(a, wgb_ref[:, cols]) * r + b_ref[:, cols])
            x = res_ref[rows, cols] + _dot(pb, wpb_ref[:, cols]) * gate
            x_ref[rows, cols] = x
            if mode == "emit":
                ssq = ssq + _emit_norm_cols(x, rows, cols, g_ref, xg_out_ref)
            elif mode == "final":
                ssq = ssq + jnp.sum(x * x, axis=-1, keepdims=True)
        if mode == "emit":
            ssq_out_ref[rows, :] = jnp.broadcast_to(ssq, (ssq.shape[0], ssq_out_ref.shape[1]))
        elif mode == "final":
            scale = lax.rsqrt(ssq * (1.0 / x_ref.shape[1]) + RMS_EPS)
            x_ref[rows, :] = x_ref[rows, :] * scale * g_ref[...]


def _row_spec(tm, k):
    return pl.BlockSpec((tm, k), lambda j, i: (i, 0))


def _col_spec(layer, k, tn, single_buffer=False):
    mode = pl.Buffered(1) if single_buffer else None
    return pl.BlockSpec((None, k, tn), lambda j, i: (layer, 0, j), pipeline_mode=mode)


def _vec_spec(tn):
    return pl.BlockSpec((1, tn), lambda j, i: (0, j))


def _tile_spec(tm, tn):
    return pl.BlockSpec((tm, tn), lambda j, i: (i, j))


def _ssq_out_spec(tm):
    return pl.BlockSpec((tm, LANES), lambda j, i: (i, j))


def _mm_params():
    return _params("parallel", "arbitrary")


def mm_qkv(a, aux, w, layer, b, s, tm=1024, tn=1536):
    m, k = a.shape
    n = w.shape[-1]
    from_x = a.dtype == _F32
    aux_spec = pl.BlockSpec((1, k), lambda j, i: (0, 0)) if from_x else _row_spec(tm, aux.shape[1])
    aux = aux.reshape(1, k) if from_x else aux
    return pl.pallas_call(
        functools.partial(_mm_qkv_kernel, k, from_x),
        grid=(n // tn, m // tm),
        in_specs=[_row_spec(tm, k), aux_spec, _col_spec(layer, k, tn),
                  _vec_spec(tn), _vec_spec(tn)],
        out_specs=_tile_spec(tm, tn),
        out_shape=jax.ShapeDtypeStruct((m, n), _BF16),
        scratch_shapes=[pltpu.VMEM((k, tn), _BF16)],
        compiler_params=_mm_params(),
        name="mm_qkv",
    )(a, aux, w, b.reshape(1, n), s.reshape(1, n))


def mm_swiglu(xg, ssq, wg, wu, layer, tm=2048, tn=512):
    m, k = xg.shape
    n = wg.shape[-1]
    return pl.pallas_call(
        functools.partial(_mm_swiglu_kernel, k),
        grid=(n // tn, m // tm),
        in_specs=[_row_spec(tm, k), _row_spec(tm, ssq.shape[1]),
                  _col_spec(layer, k, tn), _col_spec(layer, k, tn)],
        out_specs=_tile_spec(tm, tn),
        out_shape=jax.ShapeDtypeStruct((m, n), _BF16),
        scratch_shapes=[pltpu.VMEM((k, tn), _BF16), pltpu.VMEM((k, tn), _BF16)],
        compiler_params=_mm_params(),
        name="mm_swiglu",
    )(xg, ssq, wg, wu)


def _norm_out(m, n, tm, tn):
    specs = [_tile_spec(tm, tn), _tile_spec(tm, tn), _ssq_out_spec(tm)]
    shapes = [jax.ShapeDtypeStruct((m, n), _F32), jax.ShapeDtypeStruct((m, n), _BF16),
              jax.ShapeDtypeStruct((m, LANES * (n // tn)), _F32)]
    return specs, shapes


def mm_residual(a, w, layer, res, g_next, tm=512, tn=1024):
    m, k = a.shape
    n = w.shape[-1]
    out_specs, out_shape = _norm_out(m, n, tm, tn)
    if layer is None:
        w_spec, scratch = pl.BlockSpec((k, tn), lambda j, i: (0, j)), []
    else:
        w_spec, scratch = _col_spec(layer, k, tn, tn == n), [pltpu.VMEM((k, tn), _BF16)]
    return pl.pallas_call(
        _mm_residual_kernel,
        grid=(n // tn, m // tm),
        in_specs=[_row_spec(tm, k), w_spec, _tile_spec(tm, tn), _vec_spec(tn)],
        out_specs=out_specs,
        out_shape=out_shape,
        scratch_shapes=scratch,
        compiler_params=_mm_params(),
        name="mm_residual",
    )(a, w, res, g_next.reshape(1, n))


def mm_ple(xg, ssq, wg, b, p, wp, layer, res, g, mode, tm=1024, tn=1024):
    m, k = xg.shape
    n = wg.shape[-1]
    kp = p.shape[-1]
    if mode == "final":
        tm, tn = 512, n
    in_specs = [_row_spec(tm, k), _row_spec(tm, ssq.shape[1]),
                _col_spec(layer, k, tn, single_buffer=True), _vec_spec(tn),
                pl.BlockSpec((None, tm, kp), lambda j, i: (layer, i, 0)),
                _col_spec(layer, kp, tn, single_buffer=True), _tile_spec(tm, tn)]
    args = [xg, ssq, wg, b.reshape(1, n), p, wp, res]
    if mode != "plain":
        in_specs.append(_vec_spec(tn))
        args.append(g.reshape(1, n))
    if mode == "emit":
        out_specs, out_shape = _norm_out(m, n, tm, tn)
    else:
        out_specs, out_shape = _tile_spec(tm, tn), jax.ShapeDtypeStruct((m, n), _F32)
    return pl.pallas_call(
        functools.partial(_mm_ple_kernel, k, mode),
        grid=(n // tn, m // tm),
        in_specs=in_specs,
        out_specs=out_specs,
        out_shape=out_shape,
        scratch_shapes=[pltpu.VMEM((k, tn), _BF16), pltpu.VMEM((kp, tn), _BF16)],
        compiler_params=_mm_params(),
        name="mm_ple",
    )(*args)


EXT_PAD = KEY_COLS // 2
N_BIAS_ROWS = 2 * WIN_H - 1
N_QUADS = N_BIAS_ROWS - LANES // KEY_COLS + 1
GROUP_ROWS = HEADS_PER_GROUP * GRID_W
PAIRS_PER_ITER = 4


def _build_bias_quads(ext_ref, quad_ref):
    lane = lax.broadcasted_iota(jnp.int32, (GRID_W, LANES), 1)
    qc = lax.broadcasted_iota(jnp.int32, (GRID_W, LANES), 0)
    lane_j = lane // KEY_COLS
    k0 = jnp.where(qc < Q_GROUPS[1][0], Q_GROUPS[0][2],
                   jnp.where(qc < Q_GROUPS[2][0], Q_GROUPS[1][2], Q_GROUPS[2][2]))
    kc = k0 + lane % KEY_COLS
    c_start = jnp.clip(qc - WIN_W // 2, 0, GRID_W - WIN_W)
    valid = (kc >= c_start) & (kc < c_start + WIN_W)

    def head_body(h, carry):
        def rolled(dr, j):
            row = ext_ref[0, pl.ds(h * N_BIAS_ROWS + dr, 1), :]
            parts = []
            for q0, nq, k0_g in Q_GROUPS:
                shift = (KEY_COLS * j - k0_g - (WIN_W - 1) - EXT_PAD + q0) % LANES
                parts.append(pltpu.roll(jnp.broadcast_to(row, (nq, LANES)), shift, 1,
                                        stride=1, stride_axis=0))
            return jnp.concatenate(parts, axis=0)

        for d0 in range(N_QUADS):
            quad = rolled(d0, 0)
            for j in range(1, LANES // KEY_COLS):
                quad = jnp.where(lane_j == j, rolled(d0 + j, j), quad)
            quad_ref[d0, pl.ds(pl.multiple_of(h * GRID_W, GRID_W), GRID_W), :] = (
                jnp.where(valid, quad * LOG2_E, MASK_BIAS))
        return carry

    lax.fori_loop(0, HEADS_PER_GROUP, head_body, 0)


def _attn_kernel(n_side, q_ref, k_ref, v_ref, ext_ref, *refs):
    side_in, (o_ref, *side_out) = refs[:n_side], refs[n_side:2 * n_side + 1]
    quad_ref, s0_ref, s1_ref, e0_ref, e1_ref, l0_ref, l1_ref = refs[2 * n_side + 1:]
    rows = q_ref.shape[0] // GRID_W
    for wd_ref, wdb_ref in zip(side_in, side_out):
        wdb_ref[...] = wd_ref[...].astype(wdb_ref.dtype)
    lane_head = lax.broadcasted_iota(jnp.int32, (1, GROUP_LANES), 1) // HEAD_DIM
    head_masks = [lane_head == h for h in range(HEADS_PER_GROUP)]

    @pl.when(pl.program_id(1) == 0)
    def _():
        _build_bias_quads(ext_ref, quad_ref)

    def key_rows(ref, r):
        r_start = jnp.clip(r - WIN_H // 2, 0, rows - WIN_H)
        return [ref[pl.ds(pl.multiple_of((r_start + j) * GRID_W, GRID_W), GRID_W), :]
                for j in range(WIN_H)]

    def window(slabs, k0):
        return jnp.concatenate([slab[k0:k0 + KEY_COLS] for slab in slabs], axis=0)

    def score_stage(r, s_ref):
        d0 = jnp.clip(r - WIN_H // 2, 0, rows - WIN_H) - r + (WIN_H - 1)
        q_row = q_ref[pl.ds(pl.multiple_of(r * GRID_W, GRID_W), GRID_W), :].astype(_F32)
        k_slabs = key_rows(k_ref, r)
        off = 0
        for q0, nq, k0 in Q_GROUPS:
            qg = q_row[q0:q0 + nq]
            qm = jnp.concatenate([jnp.where(mask, qg, 0.0) for mask in head_masks],
                                 axis=0).astype(_BF16)
            s = lax.dot_general(qm, window(k_slabs, k0), (((1,), (1,)), ((), ())),
                                preferred_element_type=_F32)
            bias = jnp.concatenate(
                [jnp.concatenate([quad_ref[d0 + half * (LANES // KEY_COLS),
                                           pl.ds(h * GRID_W + q0, nq), :]
                                  for h in range(HEADS_PER_GROUP)], axis=0)
                 for half in range(2)], axis=1)
            s_ref[pl.ds(off, HEADS_PER_GROUP * nq), :] = s + bias
            off += HEADS_PER_GROUP * nq

    def exp_stage(s_ref, e_ref, l_ref):
        off = 0
        for _, nq, _ in Q_GROUPS:
            rows_g = pl.ds(off, HEADS_PER_GROUP * nq)
            s = s_ref[rows_g, :]
            e = jnp.exp2(s - jnp.max(s, axis=-1, keepdims=True))
            e_ref[rows_g, :] = e.astype(e_ref.dtype)
            l_ref[rows_g, :] = jnp.broadcast_to(jnp.sum(e, axis=-1, keepdims=True),
                                                (HEADS_PER_GROUP * nq, LANES))
            off += HEADS_PER_GROUP * nq

    heads_per_vreg = LANES // HEAD_DIM
    half_masks = [head_masks[h][:, :LANES] for h in range(heads_per_vreg)]

    def value_stage(r, e_ref, l_ref):
        outs = []
        v_slabs = key_rows(v_ref, r)
        off = 0
        for q0, nq, k0 in Q_GROUPS:
            rows_g = pl.ds(off, HEADS_PER_GROUP * nq)
            o_all = _dot(e_ref[rows_g, :], window(v_slabs, k0))
            sums = l_ref[rows_g, :]
            o = jnp.where(head_masks[0], o_all[:nq], 0.0)
            for h in range(1, HEADS_PER_GROUP):
                o = o + jnp.where(head_masks[h], o_all[h * nq:(h + 1) * nq], 0.0)
            den = []
            for first in range(0, HEADS_PER_GROUP, heads_per_vreg):
                d = sums[first * nq:(first + 1) * nq]
                for h in range(1, heads_per_vreg):
                    d = jnp.where(half_masks[h], sums[(first + h) * nq:(first + h + 1) * nq], d)
                den.append(d)
            outs.append(o / jnp.concatenate(den, axis=1))
            off += HEADS_PER_GROUP * nq
        o_ref[pl.ds(pl.multiple_of(r * GRID_W, GRID_W), GRID_W), :] = (
            jnp.concatenate(outs, axis=0).astype(o_ref.dtype))

    score_stage(0, s0_ref)
    score_stage(1, s1_ref)
    exp_stage(s0_ref, e0_ref, l0_ref)

    def row_pair(r):
        score_stage(r, s0_ref)
        exp_stage(s1_ref, e1_ref, l1_ref)
        value_stage(r - 2, e0_ref, l0_ref)
        score_stage(r + 1, s1_ref)
        exp_stage(s0_ref, e0_ref, l0_ref)
        value_stage(r - 1, e1_ref, l1_ref)

    n_pairs = rows // 2 - 1
    n_iters = n_pairs // PAIRS_PER_ITER

    def loop_body(k, carry):
        for pair in range(PAIRS_PER_ITER):
            row_pair(2 * (PAIRS_PER_ITER * k + pair) + 2)
        return carry

    lax.fori_loop(0, n_iters, loop_body, 0)
    for pair in range(n_iters * PAIRS_PER_ITER, n_pairs):
        row_pair(2 * pair + 2)
    exp_stage(s1_ref, e1_ref, l1_ref)
    value_stage(rows - 2, e0_ref, l0_ref)
    value_stage(rows - 1, e1_ref, l1_ref)


def _attention_bias_ext(rpb):
    ext = jnp.pad(rpb, ((0, 0), (0, 0), (EXT_PAD, EXT_PAD)), mode="edge")
    ext = jnp.pad(ext, ((0, 0), (0, 0), (0, LANES - ext.shape[-1])))
    return ext.reshape(N_HEADS // HEADS_PER_GROUP, HEADS_PER_GROUP * N_BIAS_ROWS, LANES)


def _cast_share(w, steps):
    k, n = w.shape[1:]
    assert k % (steps * BF16_SUBLANES) == 0
    return (k // steps, n), jax.ShapeDtypeStruct((k, n), _BF16)


def neighbourhood_attention(qkv, rpb, w_side, side_layers, batch, seq):
    n_hg = N_HEADS // HEADS_PER_GROUP
    side_blk, side_shape = _cast_share(w_side, n_hg * batch)
    rows = seq // GRID_W
    assert rows % 2 == 0 and rows >= WIN_H
    blk = (seq, GROUP_LANES)
    ext = _attention_bias_ext(rpb)
    score_scratch = pltpu.VMEM((GROUP_ROWS, WIN_H * KEY_COLS), _F32)
    exp_scratch = pltpu.VMEM((GROUP_ROWS, WIN_H * KEY_COLS), _BF16)
    inv_scratch = pltpu.VMEM((GROUP_ROWS, LANES), _F32)
    o, *side = pl.pallas_call(
        functools.partial(_attn_kernel, len(side_layers)),
        grid=(n_hg, batch),
        in_specs=[pl.BlockSpec(blk, lambda hg, b: (b, hg)),
                  pl.BlockSpec(blk, lambda hg, b: (b, n_hg + hg)),
                  pl.BlockSpec(blk, lambda hg, b: (b, 2 * n_hg + hg)),
                  pl.BlockSpec((1,) + ext.shape[1:], lambda hg, b: (hg, 0, 0))]
                 + [pl.BlockSpec((None,) + side_blk,
                                 functools.partial(lambda layer, hg, b: (layer, hg * batch + b, 0), layer))
                    for layer in side_layers],
        out_specs=[pl.BlockSpec(blk, lambda hg, b: (b, hg))]
                  + [pl.BlockSpec(side_blk, lambda hg, b: (hg * batch + b, 0))] * len(side_layers),
        out_shape=[jax.ShapeDtypeStruct((batch * seq, D_MODEL), _BF16)]
                  + [side_shape] * len(side_layers),
        scratch_shapes=[pltpu.VMEM((N_QUADS, GROUP_ROWS, LANES), _F32),
                        score_scratch, score_scratch, exp_scratch, exp_scratch,
                        inv_scratch, inv_scratch],
        compiler_params=_params("arbitrary", "arbitrary"),
        name="nbr_attention",
    )(qkv, qkv, qkv, ext, *([w_side] * len(side_layers)))
    return o, side


POOL_TM = 512
POOL_HALO = 8


def _pool_kernel(tiles_per_seq, x_ref, prev_ref, next_ref, g_ref, w_ref, scale_ref, gn_ref,
                 o_ref, xg_ref, ssq_ref, hext_ref, mixed_ref, wb_ref):
    i = pl.program_id(0)
    n_ext = POOL_TM + 2 * POOL_HALO
    inner = slice(POOL_HALO, POOL_HALO + POOL_TM)

    def ahead(v, s):
        return pltpu.roll(v, n_ext - s, 0)

    @pl.when(i == 0)
    def _():
        wb_ref[...] = w_ref[...].astype(wb_ref.dtype)

    def rms(v):
        return v * lax.rsqrt(jnp.mean(v * v, axis=-1, keepdims=True) + RMS_EPS) * g_ref[...]

    ti = i % tiles_per_seq
    seq = tiles_per_seq * POOL_TM
    x = x_ref[...]
    h = rms(x)
    has_prev = (ti > 0).astype(_F32)
    has_next = (ti < tiles_per_seq - 1).astype(_F32)
    hext_ref[pl.ds(0, POOL_HALO), :] = rms(prev_ref[...]) * has_prev
    hext_ref[pl.ds(POOL_HALO, POOL_TM), :] = h
    hext_ref[pl.ds(POOL_HALO + POOL_TM, POOL_HALO), :] = rms(next_ref[...]) * has_next
    t = ti * POOL_TM + lax.broadcasted_iota(jnp.int32, (POOL_TM, 1), 0)
    outs = []
    for gi, w in enumerate(POOL_WINDOWS):
        half = w // 2
        assert half <= POOL_HALO
        lo = jnp.clip(t - half, 0, seq)
        hi = jnp.clip(t + w - half, 0, seq)
        inv_cnt = 1.0 / (hi - lo).astype(_F32)
        for c in range(GROUP_CH // LANES):
            cols = slice(gi * GROUP_CH + c * LANES, gi * GROUP_CH + (c + 1) * LANES)
            hx = hext_ref[:, cols]
            run, span = hx, 1
            while span < half:
                run = run + ahead(run, span)
                span *= 2
            first = run if half == POOL_HALO else ahead(run, POOL_HALO - half)
            win = first[:POOL_TM] + run[inner]
            mixed_ref[:, cols] = (win * inv_cnt - hx[inner]).astype(mixed_ref.dtype)
        cols = slice(gi * GROUP_CH, (gi + 1) * GROUP_CH)
        outs.append(x[:, cols] + _dot(mixed_ref[:, cols], wb_ref[gi]) * scale_ref[:, cols])
    x_new = jnp.concatenate(outs, axis=1)
    o_ref[...] = x_new
    _emit_norm_inputs(x_new, gn_ref, xg_ref, ssq_ref)


def multiscale_pool(x, g, w_pool, layer, scale, g_next, seq):
    m, d = x.shape
    tiles_per_seq = seq // POOL_TM
    halo_blocks = POOL_TM // POOL_HALO
    last_halo = m // POOL_HALO - 1
    vec = pl.BlockSpec((1, d), lambda i: (0, 0))
    return pl.pallas_call(
        functools.partial(_pool_kernel, tiles_per_seq),
        grid=(m // POOL_TM,),
        in_specs=[pl.BlockSpec((POOL_TM, d), lambda i: (i, 0)),
                  pl.BlockSpec((POOL_HALO, d), lambda i: (jnp.maximum(i * halo_blocks - 1, 0), 0)),
                  pl.BlockSpec((POOL_HALO, d),
                               lambda i: (jnp.minimum((i + 1) * halo_blocks, last_halo), 0)),
                  vec,
                  pl.BlockSpec((None,) + w_pool.shape[1:], lambda i: (layer, 0, 0, 0)),
                  vec, vec],
        out_specs=[pl.BlockSpec((POOL_TM, d), lambda i: (i, 0)),
                   pl.BlockSpec((POOL_TM, d), lambda i: (i, 0)),
                   pl.BlockSpec((POOL_TM, LANES), lambda i: (i, 0))],
        out_shape=[jax.ShapeDtypeStruct((m, d), _F32), jax.ShapeDtypeStruct((m, d), _BF16),
                   jax.ShapeDtypeStruct((m, LANES), _F32)],
        scratch_shapes=[pltpu.VMEM((POOL_TM + 2 * POOL_HALO, d), _F32),
                        pltpu.VMEM((POOL_TM, d), _BF16),
                        pltpu.VMEM(w_pool.shape[1:], _BF16)],
        compiler_params=_params("arbitrary"),
        name="multiscale_pool",
    )(x, x, x, g.reshape(1, d), w_pool, scale.reshape(1, d), g_next.reshape(1, d))


def kernel(x, p, attn_norm_g, w_qkv, b_qkv, w_o, rpb, pool_norm_g, w_pool, pool_scale,
           ffn_norm_g, w_gate, w_up, w_down, ple_norm_g, w_ple_gate, b_ple_gate,
           w_ple_proj, final_norm_g):
    batch, seq, d = x.shape
    depth = p.shape[0]
    rows = seq // GRID_W
    assert d == D_MODEL and seq % GRID_W == 0 and rows >= WIN_H and seq % POOL_TM == 0
    m = batch * seq
    x = x.reshape(m, d)
    p = p.reshape(depth, m, p.shape[-1])
    q_scale = jnp.concatenate([jnp.full((d,), HEAD_DIM ** -0.5 * LOG2_E, _F32),
                               jnp.ones((2 * d,), _F32)])

    xg, ssq = x, attn_norm_g[0]
    for i in range(depth):
        j = i // 2
        if i % 2 == 0:
            qkv = mm_qkv(xg, ssq, w_qkv, j, b_qkv[j], q_scale)
            side_layers = list(range(i, min(i + 2, depth)))
            o, side = neighbourhood_attention(qkv, rpb[j], w_down, side_layers, batch, seq)
            w_down_bf16 = dict(zip(side_layers, side))
            x, xg, ssq = mm_residual(o, w_o, j, x, ffn_norm_g[i], tm=512, tn=d)
        else:
            x, xg, ssq = multiscale_pool(x, pool_norm_g[j], w_pool, j, pool_scale[j],
                                         ffn_norm_g[i], seq)
        hidden = mm_swiglu(xg, ssq, w_gate, w_up, i)
        x, xg, ssq = mm_residual(hidden, w_down_bf16[i], None, x, ple_norm_g[i])
        ple = functools.partial(mm_ple, xg, ssq, w_ple_gate, b_ple_gate[i], p, w_ple_proj, i, x)
        if i + 1 == depth:
            x = ple(final_norm_g, "final")
        elif (i + 1) % 2 == 0:
            x, xg, ssq = ple(attn_norm_g[(i + 1) // 2], "emit")
        else:
            x = ple(None, "plain")
    return x.reshape(batch, seq, d)
```
